```python
import jax, jax.numpy as jnp
from jax import lax
import numpy as np

D_MODEL = 1024
BATCH = 16
SEQ = 2048
DEPTH = 1
DEC_BATCH = 8
DEC_SEQ = 64
PAST_LEN = 4096

CHUNK = 64
EPS = 1e-6
WINDOW = 128
A_HEADS = 8
A_KV_HEADS = 2
A_HEAD_DIM = 64
A_GROUP = A_HEADS // A_KV_HEADS
A_WIDTH = A_HEADS * A_HEAD_DIM
A_KV_WIDTH = A_KV_HEADS * A_HEAD_DIM
B_HEADS = 4
B_KEY_DIM = 128
B_VAL_DIM = 128
B_KEY_WIDTH = B_HEADS * B_KEY_DIM
B_WIDTH = B_HEADS * B_VAL_DIM
MIX_WIDTH = A_WIDTH + B_WIDTH
N_IN = A_WIDTH + 2 * A_KV_WIDTH + 2 * B_KEY_WIDTH + 2 * B_WIDTH
N_GROUPS = 4
EXPERTS_PER_GROUP = 4
N_EXPERTS = N_GROUPS * EXPERTS_PER_GROUP
TOP_K = 2
D_EXPERT = 256

kernel_name = "hybrid_swa_sink_hgrn2_hmoe_stream_step"


def rmsnorm(x, g):
    xf = x.astype(jnp.float32)
    y = xf * lax.rsqrt(jnp.mean(xf * xf, axis=-1, keepdims=True) + EPS)
    return (y * g.astype(jnp.float32)).astype(x.dtype)


def alibi_slopes(n):
    return jnp.exp2(-8.0 * jnp.arange(1, n + 1, dtype=jnp.float32) / n)


def swa_sink_attention(q, k, v, k_hist, v_hist, pos0, sink):
    bsz, seq = q.shape[0], q.shape[1]
    hist = k_hist.shape[1]
    cq = min(CHUNK, seq)
    nc = seq // cq
    kp = jnp.concatenate([k_hist, k], axis=1)
    vp = jnp.concatenate([v_hist, v], axis=1)
    idx = jnp.arange(nc)[:, None] * cq + jnp.arange(hist + cq)[None, :]
    kw = kp[:, idx]
    vw = vp[:, idx]
    qb = q.reshape(bsz, nc, cq, A_KV_HEADS, A_GROUP, A_HEAD_DIM)
    s = jnp.einsum('bcqhgd,bckhd->bhgcqk', qb, kw,
                   preferred_element_type=jnp.float32) * (A_HEAD_DIM ** -0.5)
    qpos = jnp.arange(nc)[:, None] * cq + jnp.arange(cq)[None, :]
    kpos = idx - hist
    dist = jnp.abs(qpos[:, :, None] - kpos[:, None, :]).astype(jnp.float32)
    slopes = alibi_slopes(A_HEADS).reshape(A_KV_HEADS, A_GROUP)
    s = s - slopes[:, :, None, None, None] * dist
    valid = (pos0 + kpos) >= 0
    s = jnp.where(valid[:, None, :], s, -jnp.inf)
    sink_l = sink.astype(jnp.float32).reshape(A_KV_HEADS, A_GROUP)[None, :, :, None, None, None]
    m = jnp.maximum(jnp.max(s, axis=-1, keepdims=True), sink_l)
    p = jnp.exp(s - m)
    denom = jnp.sum(p, axis=-1, keepdims=True) + jnp.exp(sink_l - m)
    o = jnp.einsum('bhgcqk,bckhd->bcqhgd', (p / denom).astype(vw.dtype), vw)
    return o.reshape(bsz, seq, A_WIDTH), kp[:, -hist:], vp[:, -hist:]


def hgrn2(q, f_logit, i_in, g_in, lb, norm_g, s0):
    bsz, seq = q.shape[0], q.shape[1]
    cq = min(CHUNK, seq)
    nc = seq // cq
    f32 = jnp.float32
    qf = jax.nn.silu(q.astype(f32)).reshape(bsz, nc, cq, B_HEADS, B_KEY_DIM)
    f = lb + (1.0 - lb) * jax.nn.sigmoid(f_logit.astype(f32))
    kf = (1.0 - f).reshape(bsz, nc, cq, B_HEADS, B_KEY_DIM)
    logf = jnp.log(f).reshape(bsz, nc, cq, B_HEADS, B_KEY_DIM)
    vf = i_in.astype(f32).reshape(bsz, nc, cq, B_HEADS, B_VAL_DIM)
    cum = jnp.cumsum(logf, axis=2)
    ref = cum[:, :, cq // 2][:, :, None]
    q_in = qf * jnp.exp(cum - ref)
    k_in = kf * jnp.exp(ref - cum)
    a = jnp.einsum('bcthk,bcshk->bchts', q_in, k_in)
    a = jnp.where(jnp.tril(jnp.ones((cq, cq), dtype=bool)), a, 0.0)
    o_intra = jnp.einsum('bchts,bcshv->bcthv', a, vf)
    tot = cum[:, :, -1]
    ds = jnp.einsum('bcshk,bcshv->bchkv', kf * jnp.exp(tot[:, :, None] - cum), vf)
    decay = jnp.exp(tot)

    def step(state, inp):
        ds_c, dec_c = inp
        return dec_c[..., None] * state + ds_c, state

    s_fin, s_in = lax.scan(step, s0.astype(f32),
                           (jnp.moveaxis(ds, 1, 0), jnp.moveaxis(decay, 1, 0)))
    s_in = jnp.moveaxis(s_in, 0, 1)
    o_inter = jnp.einsum('bcthk,bchkv->bcthv', qf * jnp.exp(cum), s_in)
    o = (o_intra + o_inter).reshape(bsz, seq, B_HEADS, B_VAL_DIM)
    o = o * lax.rsqrt(jnp.mean(o * o, axis=-1, keepdims=True) + EPS)
    o = o * norm_g.astype(f32).reshape(B_HEADS, B_VAL_DIM)
    o = o.reshape(bsz, seq, B_WIDTH) * jax.nn.silu(g_in.astype(f32))
    return o.astype(q.dtype), s_fin.astype(s0.dtype)


def hier_moe(h, w_rg, b_rg, w_re, b_re, w_gate, w_up, w_down):
    f32 = jnp.float32
    g_logits = jnp.einsum('bld,dg->blg', h, w_rg, preferred_element_type=f32) + b_rg.astype(f32)
    g_prob = jax.nn.softmax(g_logits, axis=-1)
    g_idx = jnp.argmax(g_logits, axis=-1)
    p_group = jnp.max(g_prob, axis=-1, keepdims=True)
    e_all = jnp.einsum('bld,gde->blge', h, w_re, preferred_element_type=f32) + b_re.astype(f32)
    e_logits = jnp.einsum('blg,blge->ble', jax.nn.one_hot(g_idx, N_GROUPS, dtype=f32), e_all)
    e_top, e_idx = lax.top_k(e_logits, TOP_K)
    e_w = jax.nn.softmax(e_top, axis=-1) * p_group
    ids = g_idx[..., None] * EXPERTS_PER_GROUP + e_idx
    gates = jnp.sum(jax.nn.one_hot(ids, N_EXPERTS, dtype=f32) * e_w[..., None], axis=-2)
    out = jnp.zeros(h.shape, f32)
    for e in range(N_EXPERTS):
        y = (jax.nn.silu(h @ w_gate[e]) * (h @ w_up[e])) @ w_down[e]
        out = out + gates[..., e:e + 1] * y.astype(f32)
    return out.astype(h.dtype)


def layer(x, k_hist, v_hist, s0, pos0, lb, norm1_g, w_in, attn_sink, hgrn_norm_g, w_o,
          norm2_g, w_rg, b_rg, w_re, b_re, w_gate, w_up, w_down):
    bsz, seq = x.shape[0], x.shape[1]
    h = rmsnorm(x, norm1_g)
    proj = h @ w_in
    o1 = A_WIDTH
    o2 = o1 + A_KV_WIDTH
    o3 = o2 + A_KV_WIDTH
    o4 = o3 + B_KEY_WIDTH
    o5 = o4 + B_KEY_WIDTH
    o6 = o5 + B_WIDTH
    qa, ka, va, qb, fb, ib, gb = jnp.split(proj, [o1, o2, o3, o4, o5, o6], axis=-1)
    att, k_new, v_new = swa_sink_attention(
        qa.reshape(bsz, seq, A_HEADS, A_HEAD_DIM),
        ka.reshape(bsz, seq, A_KV_HEADS, A_HEAD_DIM),
        va.reshape(bsz, seq, A_KV_HEADS, A_HEAD_DIM),
        k_hist, v_hist, pos0, attn_sink)
    rec, s_new = hgrn2(qb, fb, ib, gb, lb, hgrn_norm_g, s0)
    x = x + jnp.concatenate([att, rec], axis=-1) @ w_o
    x = x + hier_moe(rmsnorm(x, norm2_g), w_rg, b_rg, w_re, b_re, w_gate, w_up, w_down)
    return x, k_new, v_new, s_new


def setup_inputs(seed: int = 0) -> dict:
    key = jax.random.key(seed)
    ks = jax.random.split(key, 24)
    f32 = jnp.float32
    w_hist = min(WINDOW, PAST_LEN)
    nrm = lambda k, shape, scale: jax.random.normal(k, shape, f32) * scale
    return {
        "x_prompt": nrm(ks[0], (BATCH, SEQ, D_MODEL), 1.0),
        "x_sample": nrm(ks[1], (DEC_BATCH, DEC_SEQ, D_MODEL), 1.0),
        "cache_k": nrm(ks[2], (DEPTH, DEC_BATCH, w_hist, A_KV_HEADS, A_HEAD_DIM), 1.0),
        "cache_v": nrm(ks[3], (DEPTH, DEC_BATCH, w_hist, A_KV_HEADS, A_HEAD_DIM), 1.0),
        "state_hgrn": nrm(ks[4], (DEPTH, DEC_BATCH, B_HEADS, B_KEY_DIM, B_VAL_DIM), 0.5),
        "norm1_g": 1.0 + nrm(ks[5], (DEPTH, D_MODEL), 0.05),
        "w_in": nrm(ks[6], (DEPTH, D_MODEL, N_IN), D_MODEL ** -0.5),
        "attn_sink": nrm(ks[7], (DEPTH, A_HEADS), 0.5),
        "lower_bounds": nrm(ks[8], (DEPTH + 1, B_KEY_WIDTH), 0.1),
        "hgrn_norm_g": 1.0 + nrm(ks[9], (DEPTH, B_WIDTH), 0.05),
        "w_o": nrm(ks[10], (DEPTH, MIX_WIDTH, D_MODEL), MIX_WIDTH ** -0.5),
        "norm2_g": 1.0 + nrm(ks[11], (DEPTH, D_MODEL), 0.05),
        "w_router_group": nrm(ks[12], (DEPTH, D_MODEL, N_GROUPS), D_MODEL ** -0.5),
        "b_router_group": nrm(ks[13], (DEPTH, N_GROUPS), 0.01),
        "w_router_expert": nrm(ks[14], (DEPTH, N_GROUPS, D_MODEL, EXPERTS_PER_GROUP), D_MODEL ** -0.5),
        "b_router_expert": nrm(ks[15], (DEPTH, N_GROUPS, EXPERTS_PER_GROUP), 0.01),
        "w_gate": nrm(ks[16], (DEPTH, N_EXPERTS, D_MODEL, D_EXPERT), D_MODEL ** -0.5),
        "w_up": nrm(ks[17], (DEPTH, N_EXPERTS, D_MODEL, D_EXPERT), D_MODEL ** -0.5),
        "w_down": nrm(ks[18], (DEPTH, N_EXPERTS, D_EXPERT, D_MODEL), D_EXPERT ** -0.5),
        "final_norm_g": 1.0 + nrm(ks[19], (D_MODEL,), 0.05),
    }


def reference(x_prompt, x_sample, cache_k, cache_v, state_hgrn, norm1_g, w_in, attn_sink,
              lower_bounds, hgrn_norm_g, w_o, norm2_g, w_router_group, b_router_group,
              w_router_expert, b_router_expert, w_gate, w_up, w_down, final_norm_g):
    lbs = jnp.cumsum(jax.nn.softmax(lower_bounds.astype(jnp.float32), axis=0), axis=0)
    bp = x_prompt.shape[0]
    w_hist = cache_k.shape[2]
    yp, ys = x_prompt, x_sample
    kp_l, vp_l, sp_l, ks_l, vs_l, ss_l = [], [], [], [], [], []
    for l in range(DEPTH):
        w = (lbs[l], norm1_g[l], w_in[l], attn_sink[l], hgrn_norm_g[l], w_o[l], norm2_g[l],
             w_router_group[l], b_router_group[l], w_router_expert[l], b_router_expert[l],
             w_gate[l], w_up[l], w_down[l])
        zk = jnp.zeros((bp, w_hist, A_KV_HEADS, A_HEAD_DIM), x_prompt.dtype)
        zs = jnp.zeros((bp, B_HEADS, B_KEY_DIM, B_VAL_DIM), state_hgrn.dtype)
        yp, kp, vp, sp = layer(yp, zk, zk, zs, 0, *w)
        ys, kn, vn, sn = layer(ys, cache_k[l], cache_v[l], state_hgrn[l], PAST_LEN, *w)
        kp_l.append(kp); vp_l.append(vp); sp_l.append(sp)
        ks_l.append(kn); vs_l.append(vn); ss_l.append(sn)
    y_prompt = rmsnorm(yp, final_norm_g)
    y_sample = rmsnorm(ys, final_norm_g)
    return (y_prompt, y_sample, jnp.stack(kp_l), jnp.stack(vp_l), jnp.stack(sp_l),
            jnp.stack(ks_l), jnp.stack(vs_l), jnp.stack(ss_l))
```

```python
import functools

import jax
import jax.numpy as jnp
from jax import lax
from jax.experimental import pallas as pl
from jax.experimental.pallas import tpu as pltpu

F32 = jnp.float32
BF16 = jnp.bfloat16

D_MODEL = 1024
CHUNK = 64
EPS = 1e-6
PAST_LEN = 4096
WINDOW = 128
A_HEADS = 8
A_KV_HEADS = 2
A_HEAD_DIM = 64
A_GROUP = A_HEADS // A_KV_HEADS
A_WIDTH = A_HEADS * A_HEAD_DIM
A_KV_WIDTH = A_KV_HEADS * A_HEAD_DIM
B_HEADS = 4
B_KEY_DIM = 128
B_VAL_DIM = 128
B_KEY_WIDTH = B_HEADS * B_KEY_DIM
B_WIDTH = B_HEADS * B_VAL_DIM
MIX_WIDTH = A_WIDTH + B_WIDTH
OFF_K = A_WIDTH
OFF_V = OFF_K + A_KV_WIDTH
OFF_QB = OFF_V + A_KV_WIDTH
OFF_FB = OFF_QB + B_KEY_WIDTH
OFF_IB = OFF_FB + B_KEY_WIDTH
OFF_GB = OFF_IB + B_WIDTH
N_IN = OFF_GB + B_WIDTH
N_GROUPS = 4
EXPERTS_PER_GROUP = 4
N_EXPERTS = N_GROUPS * EXPERTS_PER_GROUP
D_EXPERT = 256
LANES = 128
ROUTE_E0 = N_GROUPS

VMEM_LIMIT_BYTES = 48 * 1024 * 1024

NT_DIMS = (((1,), (1,)), ((), ()))
TN_DIMS = (((0,), (0,)), ((), ()))


def _rmsnorm(x, g):
    return x * lax.rsqrt(jnp.mean(x * x, axis=-1, keepdims=True) + EPS) * g


def _sigmoid(x):
    return 1.0 / (1.0 + jnp.exp(-x))


def _router_gates(logits):
    lane = lax.broadcasted_iota(jnp.int32, logits.shape, 1)
    neg = -jnp.inf
    is_g = lane < N_GROUPS
    gl = jnp.where(is_g, logits, neg)
    gmax = jnp.max(gl, axis=-1, keepdims=True)
    gidx = jnp.min(jnp.where(gl == gmax, lane, LANES), axis=-1, keepdims=True)
    p_group = 1.0 / jnp.sum(jnp.where(is_g, jnp.exp(logits - gmax), 0.0), axis=-1, keepdims=True)
    e_lo = ROUTE_E0 + EXPERTS_PER_GROUP * gidx
    in_group = (lane >= e_lo) & (lane < e_lo + EXPERTS_PER_GROUP)
    el = jnp.where(in_group, logits, neg)
    e1 = jnp.max(el, axis=-1, keepdims=True)
    i1 = jnp.min(jnp.where(el == e1, lane, LANES), axis=-1, keepdims=True)
    el2 = jnp.where(lane == i1, neg, el)
    e2 = jnp.max(el2, axis=-1, keepdims=True)
    i2 = jnp.min(jnp.where(el2 == e2, lane, LANES), axis=-1, keepdims=True)
    t = jnp.exp(e2 - e1)
    w1 = p_group / (1.0 + t)
    w2 = w1 * t
    return jnp.where(lane == i1, w1, 0.0) + jnp.where(lane == i2, w2, 0.0)


def _layer_body(pos0, tb, has_cache, *refs):
    if has_cache:
        x_ref, ck_ref, cv_ref, s0_ref, *rest = refs
    else:
        x_ref, *rest = refs
    (g1_ref, win_ref, sink_ref, lb_ref, hg_ref, wo_ref, g2_ref, wr_ref, br_ref,
     x1_ref, route_ref, kwin_ref, vwin_ref, sout_ref,
     proj_s, kbuf, vbuf, st_s, mix_s) = rest
    j = pl.program_id(1)
    nc = tb // CHUNK
    nk = WINDOW + CHUNK

    @pl.when(j == 0)
    def _init():
        if has_cache:
            kbuf[0:WINDOW, :] = ck_ref[...]
            vbuf[0:WINDOW, :] = cv_ref[...]
            for hh in range(B_HEADS):
                st_s[hh] = s0_ref[hh].T
        else:
            kbuf[0:WINDOW, :] = jnp.zeros((WINDOW, A_KV_WIDTH), F32)
            vbuf[0:WINDOW, :] = jnp.zeros((WINDOW, A_KV_WIDTH), F32)
            st_s[...] = jnp.zeros(st_s.shape, F32)

    x = x_ref[...]
    h = _rmsnorm(x, g1_ref[...]).astype(BF16)
    proj_s[...] = jnp.dot(h, win_ref[...], preferred_element_type=F32)
    kbuf[WINDOW:WINDOW + tb, :] = proj_s[:, OFF_K:OFF_K + A_KV_WIDTH]
    vbuf[WINDOW:WINDOW + tb, :] = proj_s[:, OFF_V:OFF_V + A_KV_WIDTH]

    row = lax.broadcasted_iota(jnp.int32, (A_GROUP * CHUNK, nk), 0)
    col = lax.broadcasted_iota(jnp.int32, (A_GROUP * CHUNK, nk), 1)
    dist = jnp.abs((row & (CHUNK - 1)) - (col - WINDOW)).astype(F32)
    row_head = row // CHUNK
    rowc_head = lax.broadcasted_iota(jnp.int32, (A_GROUP * CHUNK, 1), 0) // CHUNK
    bias = []
    sinkc = []
    for hk in range(A_KV_HEADS):
        slope = jnp.exp2(-(row_head + (hk * A_GROUP + 1)).astype(F32))
        bias.append(slope * dist)
        sc = jnp.zeros((A_GROUP * CHUNK, 1), F32)
        for g in range(A_GROUP):
            sc = jnp.where(rowc_head == g, sink_ref[hk * A_GROUP + g], sc)
        sinkc.append(sc)

    lbr = lb_ref[...]
    lbm = jnp.max(lbr, axis=0, keepdims=True)
    lbe = jnp.exp(lbr - lbm)
    lb = lbe[0:1, :] / jnp.sum(lbe, axis=0, keepdims=True)
    hg = hg_ref[...]
    tr = lax.broadcasted_iota(jnp.int32, (CHUNK, CHUNK), 0)
    tc = lax.broadcasted_iota(jnp.int32, (CHUNK, CHUNK), 1)
    tril = tr >= tc
    ltri = jnp.where(tril, 1.0, 0.0).astype(BF16)

    def chunk_body(c, carry):
        r0 = pl.multiple_of(c * CHUNK, CHUNK)
        rows = pl.ds(r0, CHUNK)
        valid = col >= (WINDOW - pos0) - (j * tb + c * CHUNK)
        for hk in range(A_KV_HEADS):
            qc = proj_s[rows, hk * A_GROUP * A_HEAD_DIM:(hk + 1) * A_GROUP * A_HEAD_DIM] * (A_HEAD_DIM ** -0.5)
            q4 = jnp.concatenate([qc[:, g * A_HEAD_DIM:(g + 1) * A_HEAD_DIM] for g in range(A_GROUP)],
                                 axis=0).astype(BF16)
            kw = kbuf[pl.ds(r0, nk), hk * A_HEAD_DIM:(hk + 1) * A_HEAD_DIM].astype(BF16)
            vw = vbuf[pl.ds(r0, nk), hk * A_HEAD_DIM:(hk + 1) * A_HEAD_DIM].astype(BF16)
            s = lax.dot_general(q4, kw, NT_DIMS, preferred_element_type=F32) - bias[hk]
            s = jnp.where(valid, s, -jnp.inf)
            m = jnp.maximum(jnp.max(s, axis=-1, keepdims=True), sinkc[hk])
            p = jnp.exp(s - m)
            den = jnp.sum(p, axis=-1, keepdims=True) + jnp.exp(sinkc[hk] - m)
            o = jnp.dot(p.astype(BF16), vw, preferred_element_type=F32) / den
            att = jnp.concatenate([o[g * CHUNK:(g + 1) * CHUNK] for g in range(A_GROUP)], axis=1)
            mix_s[rows, hk * A_GROUP * A_HEAD_DIM:(hk + 1) * A_GROUP * A_HEAD_DIM] = att.astype(BF16)

        qb = proj_s[rows, OFF_QB:OFF_QB + B_KEY_WIDTH]
        fl = proj_s[rows, OFF_FB:OFF_FB + B_KEY_WIDTH]
        vb = proj_s[rows, OFF_IB:OFF_IB + B_WIDTH].astype(BF16)
        gg = proj_s[rows, OFF_GB:OFF_GB + B_WIDTH]
        f = lb + (1.0 - lb) * _sigmoid(fl)
        logf = jnp.log(f)
        hi = logf.astype(BF16)
        lo = (logf - hi.astype(F32)).astype(BF16)
        cum = (jnp.dot(ltri, hi, preferred_element_type=F32) + jnp.dot(ltri, lo, preferred_element_type=F32))
        ref = cum[CHUNK // 2:CHUNK // 2 + 1, :]
        tot = cum[CHUNK - 1:CHUNK, :]
        qf = qb * _sigmoid(qb)
        kf = 1.0 - f
        q_in = (qf * jnp.exp(cum - ref)).astype(BF16)
        k_in = (kf * jnp.exp(ref - cum)).astype(BF16)
        k_tot = (kf * jnp.exp(tot - cum)).astype(BF16)
        q_cum = (qf * jnp.exp(cum)).astype(BF16)
        dec = jnp.exp(tot)
        outs = []
        for hh in range(B_HEADS):
            sl = slice(hh * B_KEY_DIM, (hh + 1) * B_KEY_DIM)
            a = lax.dot_general(q_in[:, sl], k_in[:, sl], NT_DIMS, preferred_element_type=F32)
            a = jnp.where(tril, a, 0.0).astype(BF16)
            st = st_s[hh]
            o = (jnp.dot(a, vb[:, sl], preferred_element_type=F32)
                 + lax.dot_general(q_cum[:, sl], st.astype(BF16), NT_DIMS, preferred_element_type=F32))
            ds_t = lax.dot_general(vb[:, sl], k_tot[:, sl], TN_DIMS, preferred_element_type=F32)
            st_s[hh] = st * dec[:, sl] + ds_t
            o = o * lax.rsqrt(jnp.mean(o * o, axis=-1, keepdims=True) + EPS) * hg[:, sl]
            outs.append(o)
        rec = jnp.concatenate(outs, axis=1) * (gg * _sigmoid(gg))
        mix_s[rows, A_WIDTH:A_WIDTH + B_WIDTH] = rec.astype(BF16)
        return carry

    lax.fori_loop(0, nc, chunk_body, 0)

    x1 = x + jnp.dot(mix_s[...], wo_ref[...], preferred_element_type=F32)
    x1_ref[...] = x1
    h2 = _rmsnorm(x1, g2_ref[...]).astype(BF16)
    logits = jnp.dot(h2, wr_ref[...], preferred_element_type=F32) + br_ref[...]
    route_ref[...] = _router_gates(logits)

    kt = kbuf[tb:tb + WINDOW, :]
    vt = vbuf[tb:tb + WINDOW, :]
    kbuf[0:WINDOW, :] = kt
    vbuf[0:WINDOW, :] = vt
    kwin_ref[...] = kt
    vwin_ref[...] = vt
    for hh in range(B_HEADS):
        sout_ref[hh] = st_s[hh].T


def _layer_call(x, cache, weights, pos0, tb):
    bsz, seq, _ = x.shape
    nblk = seq // tb
    has_cache = cache is not None
    g1, w_in, sink, lower_bounds, hg, w_o, g2, w_r, b_r = weights

    def const(shape):
        return pl.BlockSpec(shape, lambda b, j: (0,) * len(shape))

    in_specs = [pl.BlockSpec((None, tb, D_MODEL), lambda b, j: (b, j, 0))]
    args = [x]
    if has_cache:
        in_specs += [pl.BlockSpec((None, WINDOW, A_KV_WIDTH), lambda b, j: (b, 0, 0)),
                     pl.BlockSpec((None, WINDOW, A_KV_WIDTH), lambda b, j: (b, 0, 0)),
                     pl.BlockSpec((None, B_HEADS, B_KEY_DIM, B_VAL_DIM), lambda b, j: (b, 0, 0, 0))]
        args += list(cache)
    in_specs += [const((1, D_MODEL)), const((D_MODEL, N_IN)),
                 pl.BlockSpec(memory_space=pltpu.SMEM),
                 const((2, B_KEY_WIDTH)), const((1, B_WIDTH)), const((MIX_WIDTH, D_MODEL)),
                 const((1, D_MODEL)), const((D_MODEL, LANES)), const((1, LANES))]
    args += [g1, w_in, sink, lower_bounds, hg, w_o, g2, w_r, b_r]
    out_shape = (jax.ShapeDtypeStruct((bsz, seq, D_MODEL), F32),
                 jax.ShapeDtypeStruct((bsz, seq, LANES), F32),
                 jax.ShapeDtypeStruct((bsz, WINDOW, A_KV_WIDTH), F32),
                 jax.ShapeDtypeStruct((bsz, WINDOW, A_KV_WIDTH), F32),
                 jax.ShapeDtypeStruct((bsz, B_HEADS, B_KEY_DIM, B_VAL_DIM), F32))
    out_specs = (pl.BlockSpec((None, tb, D_MODEL), lambda b, j: (b, j, 0)),
                 pl.BlockSpec((None, tb, LANES), lambda b, j: (b, j, 0)),
                 pl.BlockSpec((None, WINDOW, A_KV_WIDTH), lambda b, j: (b, 0, 0)),
                 pl.BlockSpec((None, WINDOW, A_KV_WIDTH), lambda b, j: (b, 0, 0)),
                 pl.BlockSpec((None, B_HEADS, B_KEY_DIM, B_VAL_DIM), lambda b, j: (b, 0, 0, 0)))
    scratch = [pltpu.VMEM((tb, N_IN), F32),
               pltpu.VMEM((WINDOW + tb, A_KV_WIDTH), F32),
               pltpu.VMEM((WINDOW + tb, A_KV_WIDTH), F32),
               pltpu.VMEM((B_HEADS, B_VAL_DIM, B_KEY_DIM), F32),
               pltpu.VMEM((tb, MIX_WIDTH), BF16)]
    return pl.pallas_call(
        functools.partial(_layer_body, pos0, tb, has_cache),
        grid=(bsz, nblk),
        in_specs=in_specs,
        out_specs=out_specs,
        out_shape=out_shape,
        scratch_shapes=scratch,
        compiler_params=pltpu.CompilerParams(
            dimension_semantics=("arbitrary", "arbitrary"),
            vmem_limit_bytes=VMEM_LIMIT_BYTES),
        name="layer_prompt" if not has_cache else "layer_sample",
    )(*args)


def _moe_body(x1_ref, route_ref, g2_ref, wg_ref, wu_ref, wd_ref, gf_ref, y_ref, h2_s, acc_s):
    e = pl.program_id(1)

    @pl.when(e == 0)
    def _first():
        h2_s[...] = _rmsnorm(x1_ref[...], g2_ref[...]).astype(BF16)
        acc_s[...] = jnp.zeros(acc_s.shape, F32)

    h2 = h2_s[...]
    a = jnp.dot(h2, wg_ref[...], preferred_element_type=F32)
    u = jnp.dot(h2, wu_ref[...], preferred_element_type=F32)
    act = ((a * _sigmoid(a)) * u).astype(BF16)
    y = jnp.dot(act, wd_ref[...], preferred_element_type=F32)
    route = route_ref[...]
    lane = lax.broadcasted_iota(jnp.int32, route.shape, 1)
    gate = jnp.sum(jnp.where(lane == e + ROUTE_E0, route, 0.0), axis=-1, keepdims=True)
    acc_s[...] += gate * y

    @pl.when(e == N_EXPERTS - 1)
    def _last():
        y_ref[...] = _rmsnorm(x1_ref[...] + acc_s[...], gf_ref[...])


def _moe_call(x1, route, g2, w_gate, w_up, w_down, gf, tm):
    n = x1.shape[0]
    return pl.pallas_call(
        _moe_body,
        grid=(n // tm, N_EXPERTS),
        in_specs=[pl.BlockSpec((tm, D_MODEL), lambda i, e: (i, 0)),
                  pl.BlockSpec((tm, LANES), lambda i, e: (i, 0)),
                  pl.BlockSpec((1, D_MODEL), lambda i, e: (0, 0)),
                  pl.BlockSpec((None, D_MODEL, D_EXPERT), lambda i, e: (e, 0, 0)),
                  pl.BlockSpec((None, D_MODEL, D_EXPERT), lambda i, e: (e, 0, 0)),
                  pl.BlockSpec((None, D_EXPERT, D_MODEL), lambda i, e: (e, 0, 0)),
                  pl.BlockSpec((1, D_MODEL), lambda i, e: (0, 0))],
        out_specs=pl.BlockSpec((tm, D_MODEL), lambda i, e: (i, 0)),
        out_shape=jax.ShapeDtypeStruct((n, D_MODEL), F32),
        scratch_shapes=[pltpu.VMEM((tm, D_MODEL), BF16), pltpu.VMEM((tm, D_MODEL), F32)],
        compiler_params=pltpu.CompilerParams(
            dimension_semantics=("arbitrary", "arbitrary"),
            vmem_limit_bytes=VMEM_LIMIT_BYTES),
        name="moe_dense",
    )(x1, route, g2, w_gate, w_up, w_down, gf)


def _pick_block(seq, target):
    tb = min(seq, target)
    assert seq % tb == 0 and tb % CHUNK == 0
    return tb


def kernel(x_prompt, x_sample, cache_k, cache_v, state_hgrn, norm1_g, w_in, attn_sink, lower_bounds,
           hgrn_norm_g, w_o, norm2_g, w_router_group, b_router_group, w_router_expert, b_router_expert,
           w_gate, w_up, w_down, final_norm_g):
    depth = w_in.shape[0]
    assert depth == 1
    l = 0
    w_hist = cache_k.shape[2]
    assert w_hist == WINDOW
    w_r = jnp.concatenate(
        [w_router_group[l], jnp.transpose(w_router_expert[l], (1, 0, 2)).reshape(D_MODEL, N_EXPERTS)], axis=1)
    w_r = jnp.pad(w_r, ((0, 0), (0, LANES - w_r.shape[1]))).astype(BF16)
    b_r = jnp.concatenate([b_router_group[l], b_router_expert[l].reshape(N_EXPERTS)])
    b_r = jnp.pad(b_r, (0, LANES - b_r.shape[0])).reshape(1, LANES).astype(F32)
    weights = (norm1_g[l].reshape(1, D_MODEL), w_in[l].astype(BF16), attn_sink[l].astype(F32),
               lower_bounds.astype(F32), hgrn_norm_g[l].reshape(1, B_WIDTH), w_o[l].astype(BF16),
               norm2_g[l].reshape(1, D_MODEL), w_r, b_r)
    g2 = norm2_g[l].reshape(1, D_MODEL)
    gf = final_norm_g.reshape(1, D_MODEL)
    wg, wu, wd = w_gate[l].astype(BF16), w_up[l].astype(BF16), w_down[l].astype(BF16)

    bp, lp, _ = x_prompt.shape
    bs, ls, _ = x_sample.shape
    x1p, rp, kp, vp, sp = _layer_call(x_prompt, None, weights, 0, _pick_block(lp, 256))
    cache = (cache_k[l].reshape(bs, w_hist, A_KV_WIDTH), cache_v[l].reshape(bs, w_hist, A_KV_WIDTH), state_hgrn[l])
    x1s, rs, kn, vn, sn = _layer_call(x_sample, cache, weights, PAST_LEN, _pick_block(ls, 256))

    yp = _moe_call(x1p.reshape(bp * lp, D_MODEL), rp.reshape(bp * lp, LANES), g2, wg, wu, wd, gf,
                   _pick_block(bp * lp, 512))
    ys = _moe_call(x1s.reshape(bs * ls, D_MODEL), rs.reshape(bs * ls, LANES), g2, wg, wu, wd, gf,
                   _pick_block(bs * ls, 512))
    kv_shape = (1, -1, w_hist, A_KV_HEADS, A_HEAD_DIM)
    return (yp.reshape(bp, lp, D_MODEL), ys.reshape(bs, ls, D_MODEL),
            kp.reshape(kv_shape), vp.reshape(kv_shape), sp[None],
            kn.reshape(kv_shape), vn.reshape(kv_shape), sn[None])
```

```python
import functools

import numpy as np
import jax
import jax.numpy as jnp
from jax import lax
from jax.experimental import pallas as pl
from jax.experimental.pallas import tpu as pltpu

F32 = jnp.float32
BF16 = jnp.bfloat16
I32 = jnp.int32

D_MODEL = 1024
CHUNK = 64
EPS = 1e-6
PAST_LEN = 4096
WINDOW = 128
A_HEADS = 8
A_KV_HEADS = 2
A_HEAD_DIM = 64
A_GROUP = A_HEADS // A_KV_HEADS
A_WIDTH = A_HEADS * A_HEAD_DIM
A_KV_WIDTH = A_KV_HEADS * A_HEAD_DIM
B_HEADS = 4
B_KEY_DIM = 128
B_VAL_DIM = 128
B_KEY_WIDTH = B_HEADS * B_KEY_DIM
B_WIDTH = B_HEADS * B_VAL_DIM
MIX_WIDTH = A_WIDTH + B_WIDTH
OFF_K = A_WIDTH
OFF_V = OFF_K + A_KV_WIDTH
OFF_QB = OFF_V + A_KV_WIDTH
OFF_FB = OFF_QB + B_KEY_WIDTH
OFF_IB = OFF_FB + B_KEY_WIDTH
OFF_GB = OFF_IB + B_WIDTH
N_IN = OFF_GB + B_WIDTH
N_GROUPS = 4
EXPERTS_PER_GROUP = 4
N_EXPERTS = N_GROUPS * EXPERTS_PER_GROUP
D_EXPERT = 256
LANES = 128
ROUTE_E0 = N_GROUPS
PAIRS = [(a, b) for a in range(EXPERTS_PER_GROUP) for b in range(a + 1, EXPERTS_PER_GROUP)]
N_PAIRS = len(PAIRS)
N_CLASSES = N_GROUPS * N_PAIRS
CLS_ROWS = 32
ROW_WIDTH = D_MODEL + LANES
META_GA, META_GB, META_CLS = 0, 1, 2
MOE_TILE = 256

VMEM_LIMIT_BYTES = 48 * 1024 * 1024

NT_DIMS = (((1,), (1,)), ((), ()))
TN_DIMS = (((0,), (0,)), ((), ()))


def _rmsnorm(x, g):
    return x * lax.rsqrt(jnp.mean(x * x, axis=-1, keepdims=True) + EPS) * g


def _sigmoid(x):
    return 1.0 / (1.0 + jnp.exp(-x))


def _route(logits):
    lane = lax.broadcasted_iota(I32, logits.shape, 1)
    neg = -jnp.inf
    is_g = lane < N_GROUPS
    gl = jnp.where(is_g, logits, neg)
    gmax = jnp.max(gl, axis=-1, keepdims=True)
    gidx = jnp.min(jnp.where(gl == gmax, lane, LANES), axis=-1, keepdims=True)
    p_group = 1.0 / jnp.sum(jnp.where(is_g, jnp.exp(logits - gmax), 0.0), axis=-1, keepdims=True)
    e_lo = ROUTE_E0 + EXPERTS_PER_GROUP * gidx
    in_group = (lane >= e_lo) & (lane < e_lo + EXPERTS_PER_GROUP)
    el = jnp.where(in_group, logits, neg)
    e1 = jnp.max(el, axis=-1, keepdims=True)
    i1 = jnp.min(jnp.where(el == e1, lane, LANES), axis=-1, keepdims=True)
    el2 = jnp.where(lane == i1, neg, el)
    e2 = jnp.max(el2, axis=-1, keepdims=True)
    i2 = jnp.min(jnp.where(el2 == e2, lane, LANES), axis=-1, keepdims=True)
    t = jnp.exp(e2 - e1)
    w1 = p_group / (1.0 + t)
    w2 = w1 * t
    first_low = i1 < i2
    ea = jnp.where(first_low, i1, i2) - e_lo
    eb = jnp.where(first_low, i2, i1) - e_lo
    ga = jnp.where(first_low, w1, w2)
    gb = jnp.where(first_low, w2, w1)
    pair = ((ea * (2 * EXPERTS_PER_GROUP - 1 - ea)) >> 1) + eb - ea - 1
    return gidx * N_PAIRS + pair, ga, gb


def _layer_body(pos0, tb, has_cache, *refs):
    if has_cache:
        x_ref, ck_ref, cv_ref, s0_ref, *rest = refs
    else:
        x_ref, *rest = refs
    (cnt_in_ref, g1_ref, win_ref, sink_ref, lb_ref, hg_ref, wo_ref, g2_ref, wr_ref, br_ref,
     xr_ref, cls_ref, rank_ref, cnt_ref, kwin_ref, vwin_ref, sout_ref,
     proj_s, kbuf, vbuf, st_s, mix_s, cnt_s) = rest
    b = pl.program_id(0)
    j = pl.program_id(1)
    nc = tb // CHUNK
    nk = WINDOW + CHUNK

    @pl.when((b == 0) & (j == 0))
    def _init_counts():
        cnt_s[...] = cnt_in_ref[...]

    @pl.when(j == 0)
    def _init():
        if has_cache:
            kbuf[0:WINDOW, :] = ck_ref[...]
            vbuf[0:WINDOW, :] = cv_ref[...]
            for hh in range(B_HEADS):
                st_s[hh] = s0_ref[hh].T
        else:
            kbuf[0:WINDOW, :] = jnp.zeros((WINDOW, A_KV_WIDTH), F32)
            vbuf[0:WINDOW, :] = jnp.zeros((WINDOW, A_KV_WIDTH), F32)
            st_s[...] = jnp.zeros(st_s.shape, F32)

    x = x_ref[...]
    h = _rmsnorm(x, g1_ref[...]).astype(BF16)
    proj_s[...] = jnp.dot(h, win_ref[...], preferred_element_type=F32)
    kbuf[WINDOW:WINDOW + tb, :] = proj_s[:, OFF_K:OFF_K + A_KV_WIDTH]
    vbuf[WINDOW:WINDOW + tb, :] = proj_s[:, OFF_V:OFF_V + A_KV_WIDTH]

    row = lax.broadcasted_iota(I32, (A_GROUP * CHUNK, nk), 0)
    col = lax.broadcasted_iota(I32, (A_GROUP * CHUNK, nk), 1)
    dist = jnp.abs((row & (CHUNK - 1)) - (col - WINDOW)).astype(F32)
    row_head = row // CHUNK
    rowc_head = lax.broadcasted_iota(I32, (A_GROUP * CHUNK, 1), 0) // CHUNK
    bias = []
    sinkc = []
    for hk in range(A_KV_HEADS):
        slope = jnp.exp2(-(row_head + (hk * A_GROUP + 1)).astype(F32))
        bias.append(slope * dist)
        sc = jnp.zeros((A_GROUP * CHUNK, 1), F32)
        for g in range(A_GROUP):
            sc = jnp.where(rowc_head == g, sink_ref[hk * A_GROUP + g], sc)
        sinkc.append(sc)

    lbr = lb_ref[...]
    lbm = jnp.max(lbr, axis=0, keepdims=True)
    lbe = jnp.exp(lbr - lbm)
    lb = lbe[0:1, :] / jnp.sum(lbe, axis=0, keepdims=True)
    hg = hg_ref[...]
    tr = lax.broadcasted_iota(I32, (CHUNK, CHUNK), 0)
    tc = lax.broadcasted_iota(I32, (CHUNK, CHUNK), 1)
    tril = tr >= tc
    ltri = jnp.where(tril, 1.0, 0.0).astype(BF16)

    def chunk_body(c, carry):
        r0 = pl.multiple_of(c * CHUNK, CHUNK)
        rows = pl.ds(r0, CHUNK)
        valid = col >= (WINDOW - pos0) - (j * tb + c * CHUNK)
        for hk in range(A_KV_HEADS):
            qc = proj_s[rows, hk * A_GROUP * A_HEAD_DIM:(hk + 1) * A_GROUP * A_HEAD_DIM] * (A_HEAD_DIM ** -0.5)
            q4 = jnp.concatenate([qc[:, g * A_HEAD_DIM:(g + 1) * A_HEAD_DIM] for g in range(A_GROUP)],
                                 axis=0).astype(BF16)
            kw = kbuf[pl.ds(r0, nk), hk * A_HEAD_DIM:(hk + 1) * A_HEAD_DIM].astype(BF16)
            vw = vbuf[pl.ds(r0, nk), hk * A_HEAD_DIM:(hk + 1) * A_HEAD_DIM].astype(BF16)
            s = lax.dot_general(q4, kw, NT_DIMS, preferred_element_type=F32) - bias[hk]
            s = jnp.where(valid, s, -jnp.inf)
            m = jnp.maximum(jnp.max(s, axis=-1, keepdims=True), sinkc[hk])
            p = jnp.exp(s - m)
            den = jnp.sum(p, axis=-1, keepdims=True) + jnp.exp(sinkc[hk] - m)
            o = jnp.dot(p.astype(BF16), vw, preferred_element_type=F32) / den
            att = jnp.concatenate([o[g * CHUNK:(g + 1) * CHUNK] for g in range(A_GROUP)], axis=1)
            mix_s[rows, hk * A_GROUP * A_HEAD_DIM:(hk + 1) * A_GROUP * A_HEAD_DIM] = att.astype(BF16)

        qb = proj_s[rows, OFF_QB:OFF_QB + B_KEY_WIDTH]
        fl = proj_s[rows, OFF_FB:OFF_FB + B_KEY_WIDTH]
        vb = proj_s[rows, OFF_IB:OFF_IB + B_WIDTH].astype(BF16)
        gg = proj_s[rows, OFF_GB:OFF_GB + B_WIDTH]
        f = lb + (1.0 - lb) * _sigmoid(fl)
        logf = jnp.log(f)
        hi = logf.astype(BF16)
        lo = (logf - hi.astype(F32)).astype(BF16)
        cum = (jnp.dot(ltri, hi, preferred_element_type=F32) + jnp.dot(ltri, lo, preferred_element_type=F32))
        ref = cum[CHUNK // 2:CHUNK // 2 + 1, :]
        tot = cum[CHUNK - 1:CHUNK, :]
        qf = qb * _sigmoid(qb)
        kf = 1.0 - f
        q_in = (qf * jnp.exp(cum - ref)).astype(BF16)
        k_in = (kf * jnp.exp(ref - cum)).astype(BF16)
        k_tot = (kf * jnp.exp(tot - cum)).astype(BF16)
        q_cum = (qf * jnp.exp(cum)).astype(BF16)
        dec = jnp.exp(tot)
        outs = []
        for hh in range(B_HEADS):
            sl = slice(hh * B_KEY_DIM, (hh + 1) * B_KEY_DIM)
            a = lax.dot_general(q_in[:, sl], k_in[:, sl], NT_DIMS, preferred_element_type=F32)
            a = jnp.where(tril, a, 0.0).astype(BF16)
            st = st_s[hh]
            o = (jnp.dot(a, vb[:, sl], preferred_element_type=F32)
                 + lax.dot_general(q_cum[:, sl], st.astype(BF16), NT_DIMS, preferred_element_type=F32))
            ds_t = lax.dot_general(vb[:, sl], k_tot[:, sl], TN_DIMS, preferred_element_type=F32)
            st_s[hh] = st * dec[:, sl] + ds_t
            o = o * lax.rsqrt(jnp.mean(o * o, axis=-1, keepdims=True) + EPS) * hg[:, sl]
            outs.append(o)
        rec = jnp.concatenate(outs, axis=1) * (gg * _sigmoid(gg))
        mix_s[rows, A_WIDTH:A_WIDTH + B_WIDTH] = rec.astype(BF16)
        return carry

    lax.fori_loop(0, nc, chunk_body, 0)

    x1 = x + jnp.dot(mix_s[...], wo_ref[...], preferred_element_type=F32)
    h2 = _rmsnorm(x1, g2_ref[...]).astype(BF16)
    logits = jnp.dot(h2, wr_ref[...], preferred_element_type=F32) + br_ref[...]
    cls, ga, gb = _route(logits)
    lane = lax.broadcasted_iota(I32, (tb, LANES), 1)
    meta = jnp.where(lane == META_GA, ga,
                     jnp.where(lane == META_GB, gb, jnp.where(lane == META_CLS, cls.astype(F32), 0.0)))
    xr_ref[:, 0:D_MODEL] = x1
    xr_ref[:, D_MODEL:ROW_WIDTH] = meta

    onehot = jnp.where(lane == cls, 1.0, 0.0).astype(BF16)
    er = lax.broadcasted_iota(I32, (CLS_ROWS, LANES), 0)
    ec = lax.broadcasted_iota(I32, (CLS_ROWS, LANES), 1)
    eye = jnp.where(er == ec, 1.0, 0.0).astype(BF16)
    oht = lax.dot_general(eye, onehot, NT_DIMS, preferred_element_type=F32)
    ur = lax.broadcasted_iota(I32, (tb, tb), 0)
    uc = lax.broadcasted_iota(I32, (tb, tb), 1)
    before = jnp.where(ur < uc, 1.0, 0.0).astype(BF16)
    prefix = jnp.dot(oht.astype(BF16), before, preferred_element_type=F32)
    cnt = cnt_s[...]
    rank = jnp.sum(oht * (prefix + cnt[:, 0:1]), axis=0, keepdims=True)
    cls_row = jnp.sum(oht * lax.broadcasted_iota(I32, (CLS_ROWS, tb), 0).astype(F32), axis=0, keepdims=True)
    rank_ref[...] = rank.astype(I32)
    cls_ref[...] = cls_row.astype(I32)
    cnt_new = cnt + jnp.sum(oht, axis=1, keepdims=True)
    cnt_s[...] = cnt_new
    cnt_ref[...] = cnt_new

    kt = kbuf[tb:tb + WINDOW, :]
    vt = vbuf[tb:tb + WINDOW, :]
    kbuf[0:WINDOW, :] = kt
    vbuf[0:WINDOW, :] = vt
    kwin_ref[...] = kt
    vwin_ref[...] = vt
    for hh in range(B_HEADS):
        sout_ref[hh] = st_s[hh].T


def _layer_call(x, cache, cnt_in, weights, pos0, tb):
    bsz, seq, _ = x.shape
    nblk = seq // tb
    has_cache = cache is not None
    g1, w_in, sink, lower_bounds, hg, w_o, g2, w_r, b_r = weights

    def const(shape):
        return pl.BlockSpec(shape, lambda b, j: (0,) * len(shape))

    in_specs = [pl.BlockSpec((None, tb, D_MODEL), lambda b, j: (b, j, 0))]
    args = [x]
    if has_cache:
        in_specs += [pl.BlockSpec((None, WINDOW, A_KV_WIDTH), lambda b, j: (b, 0, 0)),
                     pl.BlockSpec((None, WINDOW, A_KV_WIDTH), lambda b, j: (b, 0, 0)),
                     pl.BlockSpec((None, B_HEADS, B_KEY_DIM, B_VAL_DIM), lambda b, j: (b, 0, 0, 0))]
        args += list(cache)
    in_specs += [const((CLS_ROWS, LANES)), const((1, D_MODEL)), const((D_MODEL, N_IN)),
                 pl.BlockSpec(memory_space=pltpu.SMEM),
                 const((2, B_KEY_WIDTH)), const((1, B_WIDTH)), const((MIX_WIDTH, D_MODEL)),
                 const((1, D_MODEL)), const((D_MODEL, LANES)), const((1, LANES))]
    args += [cnt_in, g1, w_in, sink, lower_bounds, hg, w_o, g2, w_r, b_r]
    out_shape = (jax.ShapeDtypeStruct((bsz, seq, ROW_WIDTH), F32),
                 jax.ShapeDtypeStruct((bsz * nblk, 1, tb), I32),
                 jax.ShapeDtypeStruct((bsz * nblk, 1, tb), I32),
                 jax.ShapeDtypeStruct((CLS_ROWS, LANES), F32),
                 jax.ShapeDtypeStruct((bsz, WINDOW, A_KV_WIDTH), F32),
                 jax.ShapeDtypeStruct((bsz, WINDOW, A_KV_WIDTH), F32),
                 jax.ShapeDtypeStruct((bsz, B_HEADS, B_KEY_DIM, B_VAL_DIM), F32))
    out_specs = (pl.BlockSpec((None, tb, ROW_WIDTH), lambda b, j: (b, j, 0)),
                 pl.BlockSpec((None, 1, tb), lambda b, j: (b * nblk + j, 0, 0)),
                 pl.BlockSpec((None, 1, tb), lambda b, j: (b * nblk + j, 0, 0)),
                 const((CLS_ROWS, LANES)),
                 pl.BlockSpec((None, WINDOW, A_KV_WIDTH), lambda b, j: (b, 0, 0)),
                 pl.BlockSpec((None, WINDOW, A_KV_WIDTH), lambda b, j: (b, 0, 0)),
                 pl.BlockSpec((None, B_HEADS, B_KEY_DIM, B_VAL_DIM), lambda b, j: (b, 0, 0, 0)))
    scratch = [pltpu.VMEM((tb, N_IN), F32),
               pltpu.VMEM((WINDOW + tb, A_KV_WIDTH), F32),
               pltpu.VMEM((WINDOW + tb, A_KV_WIDTH), F32),
               pltpu.VMEM((B_HEADS, B_VAL_DIM, B_KEY_DIM), F32),
               pltpu.VMEM((tb, MIX_WIDTH), BF16),
               pltpu.VMEM((CLS_ROWS, LANES), F32)]
    return pl.pallas_call(
        functools.partial(_layer_body, pos0, tb, has_cache),
        grid=(bsz, nblk),
        in_specs=in_specs,
        out_specs=out_specs,
        out_shape=out_shape,
        scratch_shapes=scratch,
        compiler_params=pltpu.CompilerParams(
            dimension_semantics=("arbitrary", "arbitrary"),
            vmem_limit_bytes=VMEM_LIMIT_BYTES),
        name="layer_prompt" if not has_cache else "layer_sample",
    )(*args)


def _row_copy(src_ref, src_row, dst_ref, dst_row, sem):
    return pltpu.make_async_copy(src_ref.at[pl.ds(src_row, 1)], dst_ref.at[pl.ds(dst_row, 1)], sem)


def _scatter_body(fill_pads, tbs, *refs):
    if fill_pads:
        pad_ref, dest_ref, x_ref, xs_ref, zrow, sem = refs
    else:
        pad_ref, dest_ref, x_ref, _, xs_ref, zrow, sem = refs
    i = pl.program_id(0)

    if fill_pads:
        @pl.when(i == 0)
        def _fill():
            zrow[...] = jnp.zeros(zrow.shape, F32)
            for c in range(N_CLASSES):
                def start(r, carry):
                    _row_copy(zrow, 0, xs_ref, r, sem).start()
                    return carry

                def wait(r, carry):
                    _row_copy(zrow, 0, xs_ref, r, sem).wait()
                    return carry

                lax.fori_loop(pad_ref[0, c], pad_ref[1, c], start, 0)
                lax.fori_loop(pad_ref[0, c], pad_ref[1, c], wait, 0)

    def start(r, carry):
        _row_copy(x_ref, r, xs_ref, dest_ref[0, r], sem).start()
        return carry

    def wait(r, carry):
        _row_copy(x_ref, r, xs_ref, dest_ref[0, r], sem).wait()
        return carry

    lax.fori_loop(0, tbs, start, 0, unroll=8)
    lax.fori_loop(0, tbs, wait, 0, unroll=8)


def _scatter_call(pads, dest, x, xs_prev, n_sorted, tbs):
    n = x.shape[0]
    fill_pads = xs_prev is None
    in_specs = [pl.BlockSpec((None, 1, tbs), lambda i, pads: (i, 0, 0), memory_space=pltpu.SMEM),
                pl.BlockSpec((tbs, ROW_WIDTH), lambda i, pads: (i, 0))]
    args = [pads, dest, x]
    aliases = {}
    if not fill_pads:
        in_specs.append(pl.BlockSpec(memory_space=pl.ANY))
        args.append(xs_prev)
        aliases = {3: 0}
    return pl.pallas_call(
        functools.partial(_scatter_body, fill_pads, tbs),
        grid_spec=pltpu.PrefetchScalarGridSpec(
            num_scalar_prefetch=1,
            grid=(n // tbs,),
            in_specs=in_specs,
            out_specs=pl.BlockSpec(memory_space=pl.ANY),
            scratch_shapes=[pltpu.VMEM((8, ROW_WIDTH), F32), pltpu.SemaphoreType.DMA(())]),
        out_shape=jax.ShapeDtypeStruct((n_sorted, ROW_WIDTH), F32),
        input_output_aliases=aliases,
        compiler_params=pltpu.CompilerParams(
            dimension_semantics=("arbitrary",), vmem_limit_bytes=VMEM_LIMIT_BYTES),
        name="scatter_first" if fill_pads else "scatter_next",
    )(*args)


def _gather_body(tbs, dest_ref, ys_ref, y_ref, sem):
    def start(r, carry):
        _row_copy(ys_ref, dest_ref[0, r], y_ref, r, sem).start()
        return carry

    def wait(r, carry):
        _row_copy(ys_ref, dest_ref[0, r], y_ref, r, sem).wait()
        return carry

    lax.fori_loop(0, tbs, start, 0, unroll=8)
    lax.fori_loop(0, tbs, wait, 0, unroll=8)


def _gather_call(dest, ys, n, tbs):
    return pl.pallas_call(
        functools.partial(_gather_body, tbs),
        grid=(n // tbs,),
        in_specs=[pl.BlockSpec((None, 1, tbs), lambda i: (i, 0, 0), memory_space=pltpu.SMEM),
                  pl.BlockSpec(memory_space=pl.ANY)],
        out_specs=pl.BlockSpec((tbs, D_MODEL), lambda i: (i, 0)),
        out_shape=jax.ShapeDtypeStruct((n, D_MODEL), F32),
        scratch_shapes=[pltpu.SemaphoreType.DMA(())],
        compiler_params=pltpu.CompilerParams(
            dimension_semantics=("arbitrary",), vmem_limit_bytes=VMEM_LIMIT_BYTES),
        name="gather_rows",
    )(dest, ys)


def _moe_body(blk_ref, ea_ref, eb_ref, nused_ref, xs_ref, g2_ref,
              wga_ref, wua_ref, wda_ref, wgb_ref, wub_ref, wdb_ref, gf_ref, ys_ref):
    i = pl.program_id(0)

    @pl.when(i < nused_ref[0])
    def _tile():
        x1 = xs_ref[:, 0:D_MODEL]
        meta = xs_ref[:, D_MODEL:ROW_WIDTH]
        h2 = _rmsnorm(x1, g2_ref[...]).astype(BF16)

        def expert(wg_ref, wu_ref, wd_ref):
            a = jnp.dot(h2, wg_ref[...], preferred_element_type=F32)
            u = jnp.dot(h2, wu_ref[...], preferred_element_type=F32)
            act = ((a * _sigmoid(a)) * u).astype(BF16)
            return jnp.dot(act, wd_ref[...], preferred_element_type=F32)

        ya = expert(wga_ref, wua_ref, wda_ref)
        yb = expert(wgb_ref, wub_ref, wdb_ref)
        moe = meta[:, META_GA:META_GA + 1] * ya + meta[:, META_GB:META_GB + 1] * yb
        ys_ref[...] = _rmsnorm(x1 + moe, gf_ref[...])


def _moe_call(tile_blk, tile_ea, tile_eb, n_used, xs, g2, wg, wu, wd, gf):
    n_sorted = xs.shape[0]
    tm = MOE_TILE

    def const(shape):
        return pl.BlockSpec(shape, lambda i, blk, ea, eb, nu: (0,) * len(shape))

    def w_a(shape):
        return pl.BlockSpec((None,) + shape, lambda i, blk, ea, eb, nu: (ea[i], 0, 0))

    def w_b(shape):
        return pl.BlockSpec((None,) + shape, lambda i, blk, ea, eb, nu: (eb[i], 0, 0))

    return pl.pallas_call(
        _moe_body,
        grid_spec=pltpu.PrefetchScalarGridSpec(
            num_scalar_prefetch=4,
            grid=(n_sorted // tm,),
            in_specs=[pl.BlockSpec((tm, ROW_WIDTH), lambda i, blk, ea, eb, nu: (blk[i], 0)),
                      const((1, D_MODEL)),
                      w_a((D_MODEL, D_EXPERT)), w_a((D_MODEL, D_EXPERT)), w_a((D_EXPERT, D_MODEL)),
                      w_b((D_MODEL, D_EXPERT)), w_b((D_MODEL, D_EXPERT)), w_b((D_EXPERT, D_MODEL)),
                      const((1, D_MODEL))],
            out_specs=pl.BlockSpec((tm, D_MODEL), lambda i, blk, ea, eb, nu: (blk[i], 0))),
        out_shape=jax.ShapeDtypeStruct((n_sorted, D_MODEL), F32),
        compiler_params=pltpu.CompilerParams(
            dimension_semantics=("arbitrary",), vmem_limit_bytes=VMEM_LIMIT_BYTES),
        name="moe_sorted",
    )(tile_blk, tile_ea, tile_eb, n_used, xs, g2, wg, wu, wd, wg, wu, wd, gf)


def _pick_block(seq, target):
    tb = min(seq, target)
    assert seq % tb == 0 and tb % CHUNK == 0
    return tb


def _plan_tiles(counts, max_tiles):
    tiles_c = (counts + MOE_TILE - 1) // MOE_TILE
    cum_tiles = jnp.cumsum(tiles_c)
    offsets = (cum_tiles - tiles_c) * MOE_TILE
    n_used = cum_tiles[-1]
    tile_blk = jnp.minimum(jnp.arange(max_tiles, dtype=I32), n_used - 1)
    tile_cls = jnp.minimum(jnp.searchsorted(cum_tiles, tile_blk, side="right"), N_CLASSES - 1).astype(I32)
    cls_ea = np.array([g * EXPERTS_PER_GROUP + a for g in range(N_GROUPS) for a, _ in PAIRS], np.int32)
    cls_eb = np.array([g * EXPERTS_PER_GROUP + b for g in range(N_GROUPS) for _, b in PAIRS], np.int32)
    pads = jnp.stack([offsets + counts, offsets + tiles_c * MOE_TILE]).astype(I32)
    return (offsets.astype(I32), tile_blk, jnp.asarray(cls_ea)[tile_cls], jnp.asarray(cls_eb)[tile_cls],
            n_used.reshape(1).astype(I32), pads)


def kernel(x_prompt, x_sample, cache_k, cache_v, state_hgrn, norm1_g, w_in, attn_sink, lower_bounds,
           hgrn_norm_g, w_o, norm2_g, w_router_group, b_router_group, w_router_expert, b_router_expert,
           w_gate, w_up, w_down, final_norm_g):
    depth = w_in.shape[0]
    assert depth == 1
    l = 0
    w_hist = cache_k.shape[2]
    assert w_hist == WINDOW
    w_r = jnp.concatenate(
        [w_router_group[l], jnp.transpose(w_router_expert[l], (1, 0, 2)).reshape(D_MODEL, N_EXPERTS)], axis=1)
    w_r = jnp.pad(w_r, ((0, 0), (0, LANES - w_r.shape[1]))).astype(BF16)
    b_r = jnp.concatenate([b_router_group[l], b_router_expert[l].reshape(N_EXPERTS)])
    b_r = jnp.pad(b_r, (0, LANES - b_r.shape[0])).reshape(1, LANES).astype(F32)
    weights = (norm1_g[l].reshape(1, D_MODEL), w_in[l].astype(BF16), attn_sink[l].astype(F32),
               lower_bounds.astype(F32), hgrn_norm_g[l].reshape(1, B_WIDTH), w_o[l].astype(BF16),
               norm2_g[l].reshape(1, D_MODEL), w_r, b_r)
    g2 = norm2_g[l].reshape(1, D_MODEL)
    gf = final_norm_g.reshape(1, D_MODEL)
    wg, wu, wd = w_gate[l].astype(BF16), w_up[l].astype(BF16), w_down[l].astype(BF16)

    bp, lp, _ = x_prompt.shape
    bs, ls, _ = x_sample.shape
    np_tok, ns_tok = bp * lp, bs * ls
    tbp, tbs = _pick_block(lp, 256), _pick_block(ls, 256)
    zero_cnt = jnp.zeros((CLS_ROWS, LANES), F32)
    xrp, clsp, rankp, cntp, kp, vp, sp = _layer_call(x_prompt, None, zero_cnt, weights, 0, tbp)
    cache = (cache_k[l].reshape(bs, w_hist, A_KV_WIDTH), cache_v[l].reshape(bs, w_hist, A_KV_WIDTH), state_hgrn[l])
    xrs, clss, ranks, cnt, kn, vn, sn = _layer_call(x_sample, cache, cntp, weights, PAST_LEN, tbs)

    max_tiles = (np_tok + ns_tok) // MOE_TILE + N_CLASSES
    counts = cnt[:N_CLASSES, 0].astype(I32)
    offsets, tile_blk, tile_ea, tile_eb, n_used, pads = _plan_tiles(counts, max_tiles)
    destp = offsets[clsp] + rankp
    dests = offsets[clss] + ranks

    n_sorted = max_tiles * MOE_TILE
    xs = _scatter_call(pads, destp, xrp.reshape(np_tok, ROW_WIDTH), None, n_sorted, tbp)
    xs = _scatter_call(pads, dests, xrs.reshape(ns_tok, ROW_WIDTH), xs, n_sorted, tbs)
    ysort = _moe_call(tile_blk, tile_ea, tile_eb, n_used, xs, g2, wg, wu, wd, gf)
    yp = _gather_call(destp, ysort, np_tok, tbp)
    ys = _gather_call(dests, ysort, ns_tok, tbs)

    kv_shape = (1, -1, w_hist, A_KV_HEADS, A_HEAD_DIM)
    return (yp.reshape(bp, lp, D_MODEL), ys.reshape(bs, ls, D_MODEL),
            kp.reshape(kv_shape), vp.reshape(kv_shape), sp[None],
            kn.reshape(kv_shape), vn.reshape(kv_shape), sn[None])
```

```python
import functools

import numpy as np
import jax
import jax.numpy as jnp
from jax import lax
from jax.experimental import pallas as pl
from jax.experimental.pallas import tpu as pltpu

F32 = jnp.float32
BF16 = jnp.bfloat16
I32 = jnp.int32

D_MODEL = 1024
CHUNK = 64
EPS = 1e-6
PAST_LEN = 4096
WINDOW = 128
A_HEADS = 8
A_KV_HEADS = 2
A_HEAD_DIM = 64
A_GROUP = A_HEADS // A_KV_HEADS
A_WIDTH = A_HEADS * A_HEAD_DIM
A_KV_WIDTH = A_KV_HEADS * A_HEAD_DIM
B_HEADS = 4
B_KEY_DIM = 128
B_VAL_DIM = 128
B_KEY_WIDTH = B_HEADS * B_KEY_DIM
B_WIDTH = B_HEADS * B_VAL_DIM
MIX_WIDTH = A_WIDTH + B_WIDTH
OFF_K = A_WIDTH
OFF_V = OFF_K + A_KV_WIDTH
OFF_QB = OFF_V + A_KV_WIDTH
OFF_FB = OFF_QB + B_KEY_WIDTH
OFF_IB = OFF_FB + B_KEY_WIDTH
OFF_GB = OFF_IB + B_WIDTH
N_IN = OFF_GB + B_WIDTH
N_GROUPS = 4
EXPERTS_PER_GROUP = 4
N_EXPERTS = N_GROUPS * EXPERTS_PER_GROUP
D_EXPERT = 256
LANES = 128
ROUTE_E0 = N_GROUPS
PAIRS = [(a, b) for a in range(EXPERTS_PER_GROUP) for b in range(a + 1, EXPERTS_PER_GROUP)]
N_PAIRS = len(PAIRS)
N_CLASSES = N_GROUPS * N_PAIRS
CLS_ROWS = 32
ROW_WIDTH = D_MODEL + LANES
META_GA, META_GB, META_CLS = 0, 1, 2
MOE_TILE = 256

VMEM_LIMIT_BYTES = 48 * 1024 * 1024

NT_DIMS = (((1,), (1,)), ((), ()))
TN_DIMS = (((0,), (0,)), ((), ()))


def _rmsnorm(x, g):
    return x * lax.rsqrt(jnp.mean(x * x, axis=-1, keepdims=True) + EPS) * g


def _sigmoid(x):
    return 1.0 / (1.0 + jnp.exp(-x))


def _route(logits):
    lane = lax.broadcasted_iota(I32, logits.shape, 1)
    neg = -jnp.inf
    is_g = lane < N_GROUPS
    gl = jnp.where(is_g, logits, neg)
    gmax = jnp.max(gl, axis=-1, keepdims=True)
    gidx = jnp.min(jnp.where(gl == gmax, lane, LANES), axis=-1, keepdims=True)
    p_group = 1.0 / jnp.sum(jnp.where(is_g, jnp.exp(logits - gmax), 0.0), axis=-1, keepdims=True)
    e_lo = ROUTE_E0 + EXPERTS_PER_GROUP * gidx
    in_group = (lane >= e_lo) & (lane < e_lo + EXPERTS_PER_GROUP)
    el = jnp.where(in_group, logits, neg)
    e1 = jnp.max(el, axis=-1, keepdims=True)
    i1 = jnp.min(jnp.where(el == e1, lane, LANES), axis=-1, keepdims=True)
    el2 = jnp.where(lane == i1, neg, el)
    e2 = jnp.max(el2, axis=-1, keepdims=True)
    i2 = jnp.min(jnp.where(el2 == e2, lane, LANES), axis=-1, keepdims=True)
    t = jnp.exp(e2 - e1)
    w1 = p_group / (1.0 + t)
    w2 = w1 * t
    first_low = i1 < i2
    ea = jnp.where(first_low, i1, i2) - e_lo
    eb = jnp.where(first_low, i2, i1) - e_lo
    ga = jnp.where(first_low, w1, w2)
    gb = jnp.where(first_low, w2, w1)
    pair = ((ea * (2 * EXPERTS_PER_GROUP - 1 - ea)) >> 1) + eb - ea - 1
    return gidx * N_PAIRS + pair, ga, gb


def _layer_body(pos0, tb, has_cache, *refs):
    if has_cache:
        x_ref, ck_ref, cv_ref, s0_ref, *rest = refs
    else:
        x_ref, *rest = refs
    (cnt_in_ref, g1_ref, win_ref, sink_ref, lb_ref, hg_ref, wo_ref, g2_ref, wr_ref, br_ref,
     xr_ref, cls_ref, rank_ref, cnt_ref, kwin_ref, vwin_ref, sout_ref,
     proj_s, kbuf, vbuf, st_s, mix_s, cnt_s) = rest
    b = pl.program_id(0)
    j = pl.program_id(1)
    nc = tb // CHUNK
    nk = WINDOW + CHUNK

    @pl.when((b == 0) & (j == 0))
    def _init_counts():
        cnt_s[...] = cnt_in_ref[...]

    @pl.when(j == 0)
    def _init():
        if has_cache:
            kbuf[0:WINDOW, :] = ck_ref[...]
            vbuf[0:WINDOW, :] = cv_ref[...]
            for hh in range(B_HEADS):
                st_s[hh] = s0_ref[hh].T
        else:
            kbuf[0:WINDOW, :] = jnp.zeros((WINDOW, A_KV_WIDTH), F32)
            vbuf[0:WINDOW, :] = jnp.zeros((WINDOW, A_KV_WIDTH), F32)
            st_s[...] = jnp.zeros(st_s.shape, F32)

    x = x_ref[...]
    h = _rmsnorm(x, g1_ref[...]).astype(BF16)
    proj_s[...] = jnp.dot(h, win_ref[...], preferred_element_type=F32)
    kbuf[WINDOW:WINDOW + tb, :] = proj_s[:, OFF_K:OFF_K + A_KV_WIDTH]
    vbuf[WINDOW:WINDOW + tb, :] = proj_s[:, OFF_V:OFF_V + A_KV_WIDTH]

    row = lax.broadcasted_iota(I32, (A_GROUP * CHUNK, nk), 0)
    col = lax.broadcasted_iota(I32, (A_GROUP * CHUNK, nk), 1)
    dist = jnp.abs((row & (CHUNK - 1)) - (col - WINDOW)).astype(F32)
    row_head = row // CHUNK
    rowc_head = lax.broadcasted_iota(I32, (A_GROUP * CHUNK, 1), 0) // CHUNK
    bias = []
    sinkc = []
    for hk in range(A_KV_HEADS):
        slope = jnp.exp2(-(row_head + (hk * A_GROUP + 1)).astype(F32))
        bias.append(slope * dist)
        sc = jnp.zeros((A_GROUP * CHUNK, 1), F32)
        for g in range(A_GROUP):
            sc = jnp.where(rowc_head == g, sink_ref[hk * A_GROUP + g], sc)
        sinkc.append(sc)

    lbr = lb_ref[...]
    lbm = jnp.max(lbr, axis=0, keepdims=True)
    lbe = jnp.exp(lbr - lbm)
    lb = lbe[0:1, :] / jnp.sum(lbe, axis=0, keepdims=True)
    hg = hg_ref[...]
    tr = lax.broadcasted_iota(I32, (CHUNK, CHUNK), 0)
    tc = lax.broadcasted_iota(I32, (CHUNK, CHUNK), 1)
    tril = tr >= tc
    ltri = jnp.where(tril, 1.0, 0.0).astype(BF16)

    def chunk_body(c, carry):
        r0 = pl.multiple_of(c * CHUNK, CHUNK)
        rows = pl.ds(r0, CHUNK)
        valid = col >= (WINDOW - pos0) - (j * tb + c * CHUNK)
        for hk in range(A_KV_HEADS):
            qc = proj_s[rows, hk * A_GROUP * A_HEAD_DIM:(hk + 1) * A_GROUP * A_HEAD_DIM] * (A_HEAD_DIM ** -0.5)
            q4 = jnp.concatenate([qc[:, g * A_HEAD_DIM:(g + 1) * A_HEAD_DIM] for g in range(A_GROUP)],
                                 axis=0).astype(BF16)
            kw = kbuf[pl.ds(r0, nk), hk * A_HEAD_DIM:(hk + 1) * A_HEAD_DIM].astype(BF16)
            vw = vbuf[pl.ds(r0, nk), hk * A_HEAD_DIM:(hk + 1) * A_HEAD_DIM].astype(BF16)
            s = lax.dot_general(q4, kw, NT_DIMS, preferred_element_type=F32) - bias[hk]
            s = jnp.where(valid, s, -jnp.inf)
            m = jnp.maximum(jnp.max(s, axis=-1, keepdims=True), sinkc[hk])
            p = jnp.exp(s - m)
            den = jnp.sum(p, axis=-1, keepdims=True) + jnp.exp(sinkc[hk] - m)
            o = jnp.dot(p.astype(BF16), vw, preferred_element_type=F32) / den
            att = jnp.concatenate([o[g * CHUNK:(g + 1) * CHUNK] for g in range(A_GROUP)], axis=1)
            mix_s[rows, hk * A_GROUP * A_HEAD_DIM:(hk + 1) * A_GROUP * A_HEAD_DIM] = att.astype(BF16)

        qb = proj_s[rows, OFF_QB:OFF_QB + B_KEY_WIDTH]
        fl = proj_s[rows, OFF_FB:OFF_FB + B_KEY_WIDTH]
        vb = proj_s[rows, OFF_IB:OFF_IB + B_WIDTH].astype(BF16)
        gg = proj_s[rows, OFF_GB:OFF_GB + B_WIDTH]
        f = lb + (1.0 - lb) * _sigmoid(fl)
        logf = jnp.log(f)
        hi = logf.astype(BF16)
        lo = (logf - hi.astype(F32)).astype(BF16)
        cum = (jnp.dot(ltri, hi, preferred_element_type=F32) + jnp.dot(ltri, lo, preferred_element_type=F32))
        ref = cum[CHUNK // 2:CHUNK // 2 + 1, :]
        tot = cum[CHUNK - 1:CHUNK, :]
        qf = qb * _sigmoid(qb)
        kf = 1.0 - f
        q_in = (qf * jnp.exp(cum - ref)).astype(BF16)
        k_in = (kf * jnp.exp(ref - cum)).astype(BF16)
        k_tot = (kf * jnp.exp(tot - cum)).astype(BF16)
        q_cum = (qf * jnp.exp(cum)).astype(BF16)
        dec = jnp.exp(tot)
        outs = []
        for hh in range(B_HEADS):
            sl = slice(hh * B_KEY_DIM, (hh + 1) * B_KEY_DIM)
            a = lax.dot_general(q_in[:, sl], k_in[:, sl], NT_DIMS, preferred_element_type=F32)
            a = jnp.where(tril, a, 0.0).astype(BF16)
            st = st_s[hh]
            o = (jnp.dot(a, vb[:, sl], preferred_element_type=F32)
                 + lax.dot_general(q_cum[:, sl], st.astype(BF16), NT_DIMS, preferred_element_type=F32))
            ds_t = lax.dot_general(vb[:, sl], k_tot[:, sl], TN_DIMS, preferred_element_type=F32)
            st_s[hh] = st * dec[:, sl] + ds_t
            o = o * lax.rsqrt(jnp.mean(o * o, axis=-1, keepdims=True) + EPS) * hg[:, sl]
            outs.append(o)
        rec = jnp.concatenate(outs, axis=1) * (gg * _sigmoid(gg))
        mix_s[rows, A_WIDTH:A_WIDTH + B_WIDTH] = rec.astype(BF16)
        return carry

    lax.fori_loop(0, nc, chunk_body, 0)

    x1 = x + jnp.dot(mix_s[...], wo_ref[...], preferred_element_type=F32)
    h2 = _rmsnorm(x1, g2_ref[...]).astype(BF16)
    logits = jnp.dot(h2, wr_ref[...], preferred_element_type=F32) + br_ref[...]
    cls, ga, gb = _route(logits)
    lane = lax.broadcasted_iota(I32, (tb, LANES), 1)
    meta = jnp.where(lane == META_GA, ga,
                     jnp.where(lane == META_GB, gb, jnp.where(lane == META_CLS, cls.astype(F32), 0.0)))
    xr_ref[:, 0:D_MODEL] = x1
    xr_ref[:, D_MODEL:ROW_WIDTH] = meta

    onehot = jnp.where(lane == cls, 1.0, 0.0).astype(BF16)
    er = lax.broadcasted_iota(I32, (CLS_ROWS, LANES), 0)
    ec = lax.broadcasted_iota(I32, (CLS_ROWS, LANES), 1)
    eye = jnp.where(er == ec, 1.0, 0.0).astype(BF16)
    oht = lax.dot_general(eye, onehot, NT_DIMS, preferred_element_type=F32)
    ur = lax.broadcasted_iota(I32, (tb, tb), 0)
    uc = lax.broadcasted_iota(I32, (tb, tb), 1)
    before = jnp.where(ur < uc, 1.0, 0.0).astype(BF16)
    prefix = jnp.dot(oht.astype(BF16), before, preferred_element_type=F32)
    cnt = cnt_s[...]
    rank = jnp.sum(oht * (prefix + cnt[:, 0:1]), axis=0, keepdims=True)
    cls_row = jnp.sum(oht * lax.broadcasted_iota(I32, (CLS_ROWS, tb), 0).astype(F32), axis=0, keepdims=True)
    rank_ref[...] = rank.astype(I32)
    cls_ref[...] = cls_row.astype(I32)
    cnt_new = cnt + jnp.sum(oht, axis=1, keepdims=True)
    cnt_s[...] = cnt_new
    cnt_ref[...] = cnt_new

    kt = kbuf[tb:tb + WINDOW, :]
    vt = vbuf[tb:tb + WINDOW, :]
    kbuf[0:WINDOW, :] = kt
    vbuf[0:WINDOW, :] = vt
    kwin_ref[...] = kt
    vwin_ref[...] = vt
    for hh in range(B_HEADS):
        sout_ref[hh] = st_s[hh].T


def _layer_call(x, cache, cnt_in, weights, pos0, tb):
    bsz, seq, _ = x.shape
    nblk = seq // tb
    has_cache = cache is not None
    g1, w_in, sink, lower_bounds, hg, w_o, g2, w_r, b_r = weights

    def const(shape):
        return pl.BlockSpec(shape, lambda b, j: (0,) * len(shape))

    in_specs = [pl.BlockSpec((None, tb, D_MODEL), lambda b, j: (b, j, 0))]
    args = [x]
    if has_cache:
        in_specs += [pl.BlockSpec((None, WINDOW, A_KV_WIDTH), lambda b, j: (b, 0, 0)),
                     pl.BlockSpec((None, WINDOW, A_KV_WIDTH), lambda b, j: (b, 0, 0)),
                     pl.BlockSpec((None, B_HEADS, B_KEY_DIM, B_VAL_DIM), lambda b, j: (b, 0, 0, 0))]
        args += list(cache)
    in_specs += [const((CLS_ROWS, LANES)), const((1, D_MODEL)), const((D_MODEL, N_IN)),
                 pl.BlockSpec(memory_space=pltpu.SMEM),
                 const((2, B_KEY_WIDTH)), const((1, B_WIDTH)), const((MIX_WIDTH, D_MODEL)),
                 const((1, D_MODEL)), const((D_MODEL, LANES)), const((1, LANES))]
    args += [cnt_in, g1, w_in, sink, lower_bounds, hg, w_o, g2, w_r, b_r]
    out_shape = (jax.ShapeDtypeStruct((bsz, seq, ROW_WIDTH), F32),
                 jax.ShapeDtypeStruct((bsz * nblk, 1, tb), I32),
                 jax.ShapeDtypeStruct((bsz * nblk, 1, tb), I32),
                 jax.ShapeDtypeStruct((CLS_ROWS, LANES), F32),
                 jax.ShapeDtypeStruct((bsz, WINDOW, A_KV_WIDTH), F32),
                 jax.ShapeDtypeStruct((bsz, WINDOW, A_KV_WIDTH), F32),
                 jax.ShapeDtypeStruct((bsz, B_HEADS, B_KEY_DIM, B_VAL_DIM), F32))
    out_specs = (pl.BlockSpec((None, tb, ROW_WIDTH), lambda b, j: (b, j, 0)),
                 pl.BlockSpec((None, 1, tb), lambda b, j: (b * nblk + j, 0, 0)),
                 pl.BlockSpec((None, 1, tb), lambda b, j: (b * nblk + j, 0, 0)),
                 const((CLS_ROWS, LANES)),
                 pl.BlockSpec((None, WINDOW, A_KV_WIDTH), lambda b, j: (b, 0, 0)),
                 pl.BlockSpec((None, WINDOW, A_KV_WIDTH), lambda b, j: (b, 0, 0)),
                 pl.BlockSpec((None, B_HEADS, B_KEY_DIM, B_VAL_DIM), lambda b, j: (b, 0, 0, 0)))
    scratch = [pltpu.VMEM((tb, N_IN), F32),
               pltpu.VMEM((WINDOW + tb, A_KV_WIDTH), F32),
               pltpu.VMEM((WINDOW + tb, A_KV_WIDTH), F32),
               pltpu.VMEM((B_HEADS, B_VAL_DIM, B_KEY_DIM), F32),
               pltpu.VMEM((tb, MIX_WIDTH), BF16),
               pltpu.VMEM((CLS_ROWS, LANES), F32)]
    return pl.pallas_call(
        functools.partial(_layer_body, pos0, tb, has_cache),
        grid=(bsz, nblk),
        in_specs=in_specs,
        out_specs=out_specs,
        out_shape=out_shape,
        scratch_shapes=scratch,
        compiler_params=pltpu.CompilerParams(
            dimension_semantics=("arbitrary", "arbitrary"),
            vmem_limit_bytes=VMEM_LIMIT_BYTES),
        name="layer_prompt" if not has_cache else "layer_sample",
    )(*args)


def _row_copy(src_ref, src_row, dst_ref, dst_row, sem):
    return pltpu.make_async_copy(src_ref.at[pl.ds(src_row, 1)], dst_ref.at[pl.ds(dst_row, 1)], sem)


def _scatter_body(fill_pads, tbs, *refs):
    if fill_pads:
        pad_ref, dest_ref, x_ref, xs_ref, zrow, sem = refs
    else:
        pad_ref, dest_ref, x_ref, _, xs_ref, zrow, sem = refs
    i = pl.program_id(0)

    if fill_pads:
        @pl.when(i == 0)
        def _fill():
            zrow[...] = jnp.zeros(zrow.shape, F32)
            for c in range(N_CLASSES):
                def start(r, carry):
                    _row_copy(zrow, 0, xs_ref, r, sem).start()
                    return carry

                def wait(r, carry):
                    _row_copy(zrow, 0, xs_ref, r, sem).wait()
                    return carry

                lax.fori_loop(pad_ref[0, c], pad_ref[1, c], start, 0)
                lax.fori_loop(pad_ref[0, c], pad_ref[1, c], wait, 0)

    for r in range(tbs):
        _row_copy(x_ref, r, xs_ref, dest_ref[0, r], sem).start(priority=r % 2)
    pltpu.make_async_copy(x_ref, xs_ref.at[pl.ds(0, tbs)], sem).wait()


def _scatter_call(pads, dest, x, xs_prev, n_sorted, tbs):
    n = x.shape[0]
    fill_pads = xs_prev is None
    in_specs = [pl.BlockSpec((None, 1, tbs), lambda i, pads: (i, 0, 0), memory_space=pltpu.SMEM),
                pl.BlockSpec((tbs, ROW_WIDTH), lambda i, pads: (i, 0))]
    args = [pads, dest, x]
    aliases = {}
    if not fill_pads:
        in_specs.append(pl.BlockSpec(memory_space=pl.ANY))
        args.append(xs_prev)
        aliases = {3: 0}
    return pl.pallas_call(
        functools.partial(_scatter_body, fill_pads, tbs),
        grid_spec=pltpu.PrefetchScalarGridSpec(
            num_scalar_prefetch=1,
            grid=(n // tbs,),
            in_specs=in_specs,
            out_specs=pl.BlockSpec(memory_space=pl.ANY),
            scratch_shapes=[pltpu.VMEM((8, ROW_WIDTH), F32), pltpu.SemaphoreType.DMA(())]),
        out_shape=jax.ShapeDtypeStruct((n_sorted, ROW_WIDTH), F32),
        input_output_aliases=aliases,
        compiler_params=pltpu.CompilerParams(
            dimension_semantics=("arbitrary",), vmem_limit_bytes=VMEM_LIMIT_BYTES),
        name="scatter_first" if fill_pads else "scatter_next",
    )(*args)


def _gather_body(tbs, dest_ref, ys_ref, y_ref, sem):
    for r in range(tbs):
        _row_copy(ys_ref, dest_ref[0, r], y_ref, r, sem).start(priority=r % 2)
    pltpu.make_async_copy(ys_ref.at[pl.ds(0, tbs)], y_ref, sem).wait()


def _gather_call(dest, ys, n, tbs):
    return pl.pallas_call(
        functools.partial(_gather_body, tbs),
        grid=(n // tbs,),
        in_specs=[pl.BlockSpec((None, 1, tbs), lambda i: (i, 0, 0), memory_space=pltpu.SMEM),
                  pl.BlockSpec(memory_space=pl.ANY)],
        out_specs=pl.BlockSpec((tbs, D_MODEL), lambda i: (i, 0)),
        out_shape=jax.ShapeDtypeStruct((n, D_MODEL), F32),
        scratch_shapes=[pltpu.SemaphoreType.DMA(())],
        compiler_params=pltpu.CompilerParams(
            dimension_semantics=("arbitrary",), vmem_limit_bytes=VMEM_LIMIT_BYTES),
        name="gather_rows",
    )(dest, ys)


def _moe_body(blk_ref, ea_ref, eb_ref, nused_ref, xs_ref, g2_ref,
              wga_ref, wua_ref, wda_ref, wgb_ref, wub_ref, wdb_ref, gf_ref, ys_ref):
    i = pl.program_id(0)

    @pl.when(i < nused_ref[0])
    def _tile():
        x1 = xs_ref[:, 0:D_MODEL]
        meta = xs_ref[:, D_MODEL:ROW_WIDTH]
        h2 = _rmsnorm(x1, g2_ref[...]).astype(BF16)

        def expert(wg_ref, wu_ref, wd_ref):
            a = jnp.dot(h2, wg_ref[...], preferred_element_type=F32)
            u = jnp.dot(h2, wu_ref[...], preferred_element_type=F32)
            act = ((a * _sigmoid(a)) * u).astype(BF16)
            return jnp.dot(act, wd_ref[...], preferred_element_type=F32)

        ya = expert(wga_ref, wua_ref, wda_ref)
        yb = expert(wgb_ref, wub_ref, wdb_ref)
        moe = meta[:, META_GA:META_GA + 1] * ya + meta[:, META_GB:META_GB + 1] * yb
        ys_ref[...] = _rmsnorm(x1 + moe, gf_ref[...])


def _moe_call(tile_blk, tile_ea, tile_eb, n_used, xs, g2, wg, wu, wd, gf):
    n_sorted = xs.shape[0]
    tm = MOE_TILE

    def const(shape):
        return pl.BlockSpec(shape, lambda i, blk, ea, eb, nu: (0,) * len(shape))

    def w_a(shape):
        return pl.BlockSpec((None,) + shape, lambda i, blk, ea, eb, nu: (ea[i], 0, 0))

    def w_b(shape):
        return pl.BlockSpec((None,) + shape, lambda i, blk, ea, eb, nu: (eb[i], 0, 0))

    return pl.pallas_call(
        _moe_body,
        grid_spec=pltpu.PrefetchScalarGridSpec(
            num_scalar_prefetch=4,
            grid=(n_sorted // tm,),
            in_specs=[pl.BlockSpec((tm, ROW_WIDTH), lambda i, blk, ea, eb, nu: (blk[i], 0)),
                      const((1, D_MODEL)),
                      w_a((D_MODEL, D_EXPERT)), w_a((D_MODEL, D_EXPERT)), w_a((D_EXPERT, D_MODEL)),
                      w_b((D_MODEL, D_EXPERT)), w_b((D_MODEL, D_EXPERT)), w_b((D_EXPERT, D_MODEL)),
                      const((1, D_MODEL))],
            out_specs=pl.BlockSpec((tm, D_MODEL), lambda i, blk, ea, eb, nu: (blk[i], 0))),
        out_shape=jax.ShapeDtypeStruct((n_sorted, D_MODEL), F32),
        compiler_params=pltpu.CompilerParams(
            dimension_semantics=("arbitrary",), vmem_limit_bytes=VMEM_LIMIT_BYTES),
        name="moe_sorted",
    )(tile_blk, tile_ea, tile_eb, n_used, xs, g2, wg, wu, wd, wg, wu, wd, gf)


def _pick_block(seq, target):
    tb = min(seq, target)
    assert seq % tb == 0 and tb % CHUNK == 0
    return tb


def _plan_tiles(counts, max_tiles):
    tiles_c = (counts + MOE_TILE - 1) // MOE_TILE
    cum_tiles = jnp.cumsum(tiles_c)
    offsets = (cum_tiles - tiles_c) * MOE_TILE
    n_used = cum_tiles[-1]
    tile_blk = jnp.minimum(jnp.arange(max_tiles, dtype=I32), n_used - 1)
    tile_cls = jnp.sum(tile_blk[:, None] >= cum_tiles[None, :], axis=1)
    tile_cls = jnp.minimum(tile_cls, N_CLASSES - 1).astype(I32)
    cls_ea = np.array([g * EXPERTS_PER_GROUP + a for g in range(N_GROUPS) for a, _ in PAIRS], np.int32)
    cls_eb = np.array([g * EXPERTS_PER_GROUP + b for g in range(N_GROUPS) for _, b in PAIRS], np.int32)
    pads = jnp.stack([offsets + counts, offsets + tiles_c * MOE_TILE]).astype(I32)
    return (offsets.astype(I32), tile_blk, _lookup(cls_ea, tile_cls), _lookup(cls_eb, tile_cls),
            n_used.reshape(1).astype(I32), pads)


def _lookup(table, idx):
    table = jnp.asarray(table, I32)
    k = jnp.arange(table.shape[0], dtype=I32)
    return jnp.sum(jnp.where(idx[..., None] == k, table, 0), axis=-1).astype(I32)


def kernel(x_prompt, x_sample, cache_k, cache_v, state_hgrn, norm1_g, w_in, attn_sink, lower_bounds,
           hgrn_norm_g, w_o, norm2_g, w_router_group, b_router_group, w_router_expert, b_router_expert,
           w_gate, w_up, w_down, final_norm_g):
    depth = w_in.shape[0]
    assert depth == 1
    l = 0
    w_hist = cache_k.shape[2]
    assert w_hist == WINDOW
    w_r = jnp.concatenate(
        [w_router_group[l], jnp.transpose(w_router_expert[l], (1, 0, 2)).reshape(D_MODEL, N_EXPERTS)], axis=1)
    w_r = jnp.pad(w_r, ((0, 0), (0, LANES - w_r.shape[1]))).astype(BF16)
    b_r = jnp.concatenate([b_router_group[l], b_router_expert[l].reshape(N_EXPERTS)])
    b_r = jnp.pad(b_r, (0, LANES - b_r.shape[0])).reshape(1, LANES).astype(F32)
    weights = (norm1_g[l].reshape(1, D_MODEL), w_in[l].astype(BF16), attn_sink[l].astype(F32),
               lower_bounds.astype(F32), hgrn_norm_g[l].reshape(1, B_WIDTH), w_o[l].astype(BF16),
               norm2_g[l].reshape(1, D_MODEL), w_r, b_r)
    g2 = norm2_g[l].reshape(1, D_MODEL)
    gf = final_norm_g.reshape(1, D_MODEL)
    wg, wu, wd = w_gate[l].astype(BF16), w_up[l].astype(BF16), w_down[l].astype(BF16)

    bp, lp, _ = x_prompt.shape
    bs, ls, _ = x_sample.shape
    np_tok, ns_tok = bp * lp, bs * ls
    tbp, tbs = _pick_block(lp, 256), _pick_block(ls, 256)
    zero_cnt = jnp.zeros((CLS_ROWS, LANES), F32)
    xrp, clsp, rankp, cntp, kp, vp, sp = _layer_call(x_prompt, None, zero_cnt, weights, 0, tbp)
    cache = (cache_k[l].reshape(bs, w_hist, A_KV_WIDTH), cache_v[l].reshape(bs, w_hist, A_KV_WIDTH), state_hgrn[l])
    xrs, clss, ranks, cnt, kn, vn, sn = _layer_call(x_sample, cache, cntp, weights, PAST_LEN, tbs)

    max_tiles = (np_tok + ns_tok) // MOE_TILE + N_CLASSES
    counts = cnt[:N_CLASSES, 0].astype(I32)
    offsets, tile_blk, tile_ea, tile_eb, n_used, pads = _plan_tiles(counts, max_tiles)
    destp = _lookup(offsets, clsp) + rankp
    dests = _lookup(offsets, clss) + ranks

    n_sorted = max_tiles * MOE_TILE
    xs = _scatter_call(pads, destp, xrp.reshape(np_tok, ROW_WIDTH), None, n_sorted, tbp)
    xs = _scatter_call(pads, dests, xrs.reshape(ns_tok, ROW_WIDTH), xs, n_sorted, tbs)
    ysort = _moe_call(tile_blk, tile_ea, tile_eb, n_used, xs, g2, wg, wu, wd, gf)
    yp = _gather_call(destp, ysort, np_tok, tbp)
    ys = _gather_call(dests, ysort, ns_tok, tbs)

    kv_shape = (1, -1, w_hist, A_KV_HEADS, A_HEAD_DIM)
    return (yp.reshape(bp, lp, D_MODEL), ys.reshape(bs, ls, D_MODEL),
            kp.reshape(kv_shape), vp.reshape(kv_shape), sp[None],
            kn.reshape(kv_shape), vn.reshape(kv_shape), sn[None])
```

```python
import functools

import numpy as np
import jax
import jax.numpy as jnp
from jax import lax
from jax.experimental import pallas as pl
from jax.experimental.pallas import tpu as pltpu

F32 = jnp.float32
BF16 = jnp.bfloat16
I32 = jnp.int32

D_MODEL = 1024
CHUNK = 64
EPS = 1e-6
PAST_LEN = 4096
WINDOW = 128
A_HEADS = 8
A_KV_HEADS = 2
A_HEAD_DIM = 64
A_GROUP = A_HEADS // A_KV_HEADS
A_WIDTH = A_HEADS * A_HEAD_DIM
A_KV_WIDTH = A_KV_HEADS * A_HEAD_DIM
B_HEADS = 4
B_KEY_DIM = 128
B_VAL_DIM = 128
B_KEY_WIDTH = B_HEADS * B_KEY_DIM
B_WIDTH = B_HEADS * B_VAL_DIM
MIX_WIDTH = A_WIDTH + B_WIDTH
OFF_K = A_WIDTH
OFF_V = OFF_K + A_KV_WIDTH
OFF_QB = OFF_V + A_KV_WIDTH
OFF_FB = OFF_QB + B_KEY_WIDTH
OFF_IB = OFF_FB + B_KEY_WIDTH
OFF_GB = OFF_IB + B_WIDTH
N_IN = OFF_GB + B_WIDTH
N_GROUPS = 4
EXPERTS_PER_GROUP = 4
N_EXPERTS = N_GROUPS * EXPERTS_PER_GROUP
D_EXPERT = 256
LANES = 128
ROUTE_E0 = N_GROUPS
PAIRS = [(a, b) for a in range(EXPERTS_PER_GROUP) for b in range(a + 1, EXPERTS_PER_GROUP)]
N_PAIRS = len(PAIRS)
N_CLASSES = N_GROUPS * N_PAIRS
CLS_ROWS = 32
ROW_WIDTH = D_MODEL + LANES
META_GA, META_GB, META_CLS = 0, 1, 2
LAYER_BLOCK = 512
CHUNK_UNROLL = 4
MOE_TILE = 256

VMEM_LIMIT_BYTES = 48 * 1024 * 1024

NT_DIMS = (((1,), (1,)), ((), ()))
TN_DIMS = (((0,), (0,)), ((), ()))


def _rmsnorm(x, g):
    return x * lax.rsqrt(jnp.mean(x * x, axis=-1, keepdims=True) + EPS) * g


def _sigmoid(x):
    return 1.0 / (1.0 + jnp.exp(-x))


def _route(logits):
    lane = lax.broadcasted_iota(I32, logits.shape, 1)
    neg = -jnp.inf
    is_g = lane < N_GROUPS
    gl = jnp.where(is_g, logits, neg)
    gmax = jnp.max(gl, axis=-1, keepdims=True)
    gidx = jnp.min(jnp.where(gl == gmax, lane, LANES), axis=-1, keepdims=True)
    p_group = 1.0 / jnp.sum(jnp.where(is_g, jnp.exp(logits - gmax), 0.0), axis=-1, keepdims=True)
    e_lo = ROUTE_E0 + EXPERTS_PER_GROUP * gidx
    in_group = (lane >= e_lo) & (lane < e_lo + EXPERTS_PER_GROUP)
    el = jnp.where(in_group, logits, neg)
    e1 = jnp.max(el, axis=-1, keepdims=True)
    i1 = jnp.min(jnp.where(el == e1, lane, LANES), axis=-1, keepdims=True)
    el2 = jnp.where(lane == i1, neg, el)
    e2 = jnp.max(el2, axis=-1, keepdims=True)
    i2 = jnp.min(jnp.where(el2 == e2, lane, LANES), axis=-1, keepdims=True)
    t = jnp.exp(e2 - e1)
    w1 = p_group / (1.0 + t)
    w2 = w1 * t
    first_low = i1 < i2
    ea = jnp.where(first_low, i1, i2) - e_lo
    eb = jnp.where(first_low, i2, i1) - e_lo
    ga = jnp.where(first_low, w1, w2)
    gb = jnp.where(first_low, w2, w1)
    pair = ((ea * (2 * EXPERTS_PER_GROUP - 1 - ea)) >> 1) + eb - ea - 1
    return gidx * N_PAIRS + pair, ga, gb


def _layer_body(pos0, tb, has_cache, *refs):
    if has_cache:
        x_ref, ck_ref, cv_ref, s0_ref, *rest = refs
    else:
        x_ref, *rest = refs
    (cnt_in_ref, g1_ref, win_ref, sink_ref, lb_ref, hg_ref, wo_ref, g2_ref, wr_ref, br_ref,
     xr_ref, cls_ref, rank_ref, cnt_ref, kwin_ref, vwin_ref, sout_ref,
     proj_s, kbuf, vbuf, st_s, mix_s, cnt_s) = rest
    b = pl.program_id(0)
    j = pl.program_id(1)
    nc = tb // CHUNK
    nk = WINDOW + CHUNK

    @pl.when((b == 0) & (j == 0))
    def _init_counts():
        cnt_s[...] = cnt_in_ref[...]

    @pl.when(j == 0)
    def _init():
        if has_cache:
            kbuf[0:WINDOW, :] = ck_ref[...]
            vbuf[0:WINDOW, :] = cv_ref[...]
            for hh in range(B_HEADS):
                st_s[hh] = s0_ref[hh].T
        else:
            kbuf[0:WINDOW, :] = jnp.zeros((WINDOW, A_KV_WIDTH), F32)
            vbuf[0:WINDOW, :] = jnp.zeros((WINDOW, A_KV_WIDTH), F32)
            st_s[...] = jnp.zeros(st_s.shape, F32)

    x = x_ref[...]
    h = _rmsnorm(x, g1_ref[...]).astype(BF16)
    proj_s[...] = jnp.dot(h, win_ref[...], preferred_element_type=F32)
    kbuf[WINDOW:WINDOW + tb, :] = proj_s[:, OFF_K:OFF_K + A_KV_WIDTH]
    vbuf[WINDOW:WINDOW + tb, :] = proj_s[:, OFF_V:OFF_V + A_KV_WIDTH]

    row = lax.broadcasted_iota(I32, (A_GROUP * CHUNK, nk), 0)
    col = lax.broadcasted_iota(I32, (A_GROUP * CHUNK, nk), 1)
    dist = jnp.abs((row & (CHUNK - 1)) - (col - WINDOW)).astype(F32)
    row_head = row // CHUNK
    rowc_head = lax.broadcasted_iota(I32, (A_GROUP * CHUNK, 1), 0) // CHUNK
    bias = []
    sinkc = []
    for hk in range(A_KV_HEADS):
        slope = jnp.exp2(-(row_head + (hk * A_GROUP + 1)).astype(F32))
        bias.append(slope * dist)
        sc = jnp.zeros((A_GROUP * CHUNK, 1), F32)
        for g in range(A_GROUP):
            sc = jnp.where(rowc_head == g, sink_ref[hk * A_GROUP + g], sc)
        sinkc.append(sc)

    lbr = lb_ref[...]
    lbm = jnp.max(lbr, axis=0, keepdims=True)
    lbe = jnp.exp(lbr - lbm)
    lb = lbe[0:1, :] / jnp.sum(lbe, axis=0, keepdims=True)
    hg = hg_ref[...]
    tr = lax.broadcasted_iota(I32, (CHUNK, CHUNK), 0)
    tc = lax.broadcasted_iota(I32, (CHUNK, CHUNK), 1)
    tril = tr >= tc
    ltri = jnp.where(tril, 1.0, 0.0).astype(BF16)

    def chunk_body(c, carry):
        r0 = pl.multiple_of(c * CHUNK, CHUNK)
        rows = pl.ds(r0, CHUNK)
        valid = col >= (WINDOW - pos0) - (j * tb + c * CHUNK)
        for hk in range(A_KV_HEADS):
            qc = proj_s[rows, hk * A_GROUP * A_HEAD_DIM:(hk + 1) * A_GROUP * A_HEAD_DIM] * (A_HEAD_DIM ** -0.5)
            q4 = jnp.concatenate([qc[:, g * A_HEAD_DIM:(g + 1) * A_HEAD_DIM] for g in range(A_GROUP)],
                                 axis=0).astype(BF16)
            kw = kbuf[pl.ds(r0, nk), hk * A_HEAD_DIM:(hk + 1) * A_HEAD_DIM].astype(BF16)
            vw = vbuf[pl.ds(r0, nk), hk * A_HEAD_DIM:(hk + 1) * A_HEAD_DIM].astype(BF16)
            s = lax.dot_general(q4, kw, NT_DIMS, preferred_element_type=F32) - bias[hk]
            s = jnp.where(valid, s, -jnp.inf)
            m = jnp.maximum(jnp.max(s, axis=-1, keepdims=True), sinkc[hk])
            p = jnp.exp(s - m)
            den = jnp.sum(p, axis=-1, keepdims=True) + jnp.exp(sinkc[hk] - m)
            o = jnp.dot(p.astype(BF16), vw, preferred_element_type=F32) / den
            att = jnp.concatenate([o[g * CHUNK:(g + 1) * CHUNK] for g in range(A_GROUP)], axis=1)
            mix_s[rows, hk * A_GROUP * A_HEAD_DIM:(hk + 1) * A_GROUP * A_HEAD_DIM] = att.astype(BF16)

        qb = proj_s[rows, OFF_QB:OFF_QB + B_KEY_WIDTH]
        fl = proj_s[rows, OFF_FB:OFF_FB + B_KEY_WIDTH]
        vb = proj_s[rows, OFF_IB:OFF_IB + B_WIDTH].astype(BF16)
        gg = proj_s[rows, OFF_GB:OFF_GB + B_WIDTH]
        f = lb + (1.0 - lb) * _sigmoid(fl)
        logf = jnp.log(f)
        hi = logf.astype(BF16)
        lo = (logf - hi.astype(F32)).astype(BF16)
        cum = (jnp.dot(ltri, hi, preferred_element_type=F32) + jnp.dot(ltri, lo, preferred_element_type=F32))
        ref = cum[CHUNK // 2:CHUNK // 2 + 1, :]
        tot = cum[CHUNK - 1:CHUNK, :]
        qf = qb * _sigmoid(qb)
        kf = 1.0 - f
        q_in = (qf * jnp.exp(cum - ref)).astype(BF16)
        k_in = (kf * jnp.exp(ref - cum)).astype(BF16)
        k_tot = (kf * jnp.exp(tot - cum)).astype(BF16)
        q_cum = (qf * jnp.exp(cum)).astype(BF16)
        dec = jnp.exp(tot)
        outs = []
        for hh in range(B_HEADS):
            sl = slice(hh * B_KEY_DIM, (hh + 1) * B_KEY_DIM)
            a = lax.dot_general(q_in[:, sl], k_in[:, sl], NT_DIMS, preferred_element_type=F32)
            a = jnp.where(tril, a, 0.0).astype(BF16)
            st = st_s[hh]
            o = (jnp.dot(a, vb[:, sl], preferred_element_type=F32)
                 + lax.dot_general(q_cum[:, sl], st.astype(BF16), NT_DIMS, preferred_element_type=F32))
            ds_t = lax.dot_general(vb[:, sl], k_tot[:, sl], TN_DIMS, preferred_element_type=F32)
            st_s[hh] = st * dec[:, sl] + ds_t
            o = o * lax.rsqrt(jnp.mean(o * o, axis=-1, keepdims=True) + EPS) * hg[:, sl]
            outs.append(o)
        rec = jnp.concatenate(outs, axis=1) * (gg * _sigmoid(gg))
        mix_s[rows, A_WIDTH:A_WIDTH + B_WIDTH] = rec.astype(BF16)
        return carry

    lax.fori_loop(0, nc, chunk_body, 0, unroll=min(nc, CHUNK_UNROLL))

    x1 = x + jnp.dot(mix_s[...], wo_ref[...], preferred_element_type=F32)
    h2 = _rmsnorm(x1, g2_ref[...]).astype(BF16)
    logits = jnp.dot(h2, wr_ref[...], preferred_element_type=F32) + br_ref[...]
    cls, ga, gb = _route(logits)
    lane = lax.broadcasted_iota(I32, (tb, LANES), 1)
    meta = jnp.where(lane == META_GA, ga,
                     jnp.where(lane == META_GB, gb, jnp.where(lane == META_CLS, cls.astype(F32), 0.0)))
    xr_ref[:, 0:D_MODEL] = x1
    xr_ref[:, D_MODEL:ROW_WIDTH] = meta

    onehot = jnp.where(lane == cls, 1.0, 0.0).astype(BF16)
    er = lax.broadcasted_iota(I32, (CLS_ROWS, LANES), 0)
    ec = lax.broadcasted_iota(I32, (CLS_ROWS, LANES), 1)
    eye = jnp.where(er == ec, 1.0, 0.0).astype(BF16)
    oht = lax.dot_general(eye, onehot, NT_DIMS, preferred_element_type=F32)
    ur = lax.broadcasted_iota(I32, (tb, tb), 0)
    uc = lax.broadcasted_iota(I32, (tb, tb), 1)
    before = jnp.where(ur < uc, 1.0, 0.0).astype(BF16)
    prefix = jnp.dot(oht.astype(BF16), before, preferred_element_type=F32)
    cnt = cnt_s[...]
    rank = jnp.sum(oht * (prefix + cnt[:, 0:1]), axis=0, keepdims=True)
    cls_row = jnp.sum(oht * lax.broadcasted_iota(I32, (CLS_ROWS, tb), 0).astype(F32), axis=0, keepdims=True)
    rank_ref[...] = rank.astype(I32)
    cls_ref[...] = cls_row.astype(I32)
    cnt_new = cnt + jnp.sum(oht, axis=1, keepdims=True)
    cnt_s[...] = cnt_new
    cnt_ref[...] = cnt_new

    kt = kbuf[tb:tb + WINDOW, :]
    vt = vbuf[tb:tb + WINDOW, :]
    kbuf[0:WINDOW, :] = kt
    vbuf[0:WINDOW, :] = vt
    kwin_ref[...] = kt
    vwin_ref[...] = vt
    for hh in range(B_HEADS):
        sout_ref[hh] = st_s[hh].T


def _layer_call(x, cache, cnt_in, weights, pos0, tb):
    bsz, seq, _ = x.shape
    nblk = seq // tb
    has_cache = cache is not None
    g1, w_in, sink, lower_bounds, hg, w_o, g2, w_r, b_r = weights

    def const(shape):
        return pl.BlockSpec(shape, lambda b, j: (0,) * len(shape))

    in_specs = [pl.BlockSpec((None, tb, D_MODEL), lambda b, j: (b, j, 0))]
    args = [x]
    if has_cache:
        in_specs += [pl.BlockSpec((None, WINDOW, A_KV_WIDTH), lambda b, j: (b, 0, 0)),
                     pl.BlockSpec((None, WINDOW, A_KV_WIDTH), lambda b, j: (b, 0, 0)),
                     pl.BlockSpec((None, B_HEADS, B_KEY_DIM, B_VAL_DIM), lambda b, j: (b, 0, 0, 0))]
        args += list(cache)
    in_specs += [const((CLS_ROWS, LANES)), const((1, D_MODEL)), const((D_MODEL, N_IN)),
                 pl.BlockSpec(memory_space=pltpu.SMEM),
                 const((2, B_KEY_WIDTH)), const((1, B_WIDTH)), const((MIX_WIDTH, D_MODEL)),
                 const((1, D_MODEL)), const((D_MODEL, LANES)), const((1, LANES))]
    args += [cnt_in, g1, w_in, sink, lower_bounds, hg, w_o, g2, w_r, b_r]
    out_shape = (jax.ShapeDtypeStruct((bsz, seq, ROW_WIDTH), F32),
                 jax.ShapeDtypeStruct((bsz * nblk, 1, tb), I32),
                 jax.ShapeDtypeStruct((bsz * nblk, 1, tb), I32),
                 jax.ShapeDtypeStruct((CLS_ROWS, LANES), F32),
                 jax.ShapeDtypeStruct((bsz, WINDOW, A_KV_WIDTH), F32),
                 jax.ShapeDtypeStruct((bsz, WINDOW, A_KV_WIDTH), F32),
                 jax.ShapeDtypeStruct((bsz, B_HEADS, B_KEY_DIM, B_VAL_DIM), F32))
    out_specs = (pl.BlockSpec((None, tb, ROW_WIDTH), lambda b, j: (b, j, 0)),
                 pl.BlockSpec((None, 1, tb), lambda b, j: (b * nblk + j, 0, 0)),
                 pl.BlockSpec((None, 1, tb), lambda b, j: (b * nblk + j, 0, 0)),
                 const((CLS_ROWS, LANES)),
                 pl.BlockSpec((None, WINDOW, A_KV_WIDTH), lambda b, j: (b, 0, 0)),
                 pl.BlockSpec((None, WINDOW, A_KV_WIDTH), lambda b, j: (b, 0, 0)),
                 pl.BlockSpec((None, B_HEADS, B_KEY_DIM, B_VAL_DIM), lambda b, j: (b, 0, 0, 0)))
    scratch = [pltpu.VMEM((tb, N_IN), F32),
               pltpu.VMEM((WINDOW + tb, A_KV_WIDTH), F32),
               pltpu.VMEM((WINDOW + tb, A_KV_WIDTH), F32),
               pltpu.VMEM((B_HEADS, B_VAL_DIM, B_KEY_DIM), F32),
               pltpu.VMEM((tb, MIX_WIDTH), BF16),
               pltpu.VMEM((CLS_ROWS, LANES), F32)]
    return pl.pallas_call(
        functools.partial(_layer_body, pos0, tb, has_cache),
        grid=(bsz, nblk),
        in_specs=in_specs,
        out_specs=out_specs,
        out_shape=out_shape,
        scratch_shapes=scratch,
        compiler_params=pltpu.CompilerParams(
            dimension_semantics=("arbitrary", "arbitrary"),
            vmem_limit_bytes=VMEM_LIMIT_BYTES),
        name="layer_prompt" if not has_cache else "layer_sample",
    )(*args)


def _row_copy(src_ref, src_row, dst_ref, dst_row, sem):
    return pltpu.make_async_copy(src_ref.at[pl.ds(src_row, 1)], dst_ref.at[pl.ds(dst_row, 1)], sem)


def _scatter_body(fill_pads, tbs, *refs):
    if fill_pads:
        pad_ref, dest_ref, x_ref, xs_ref, zrow, sem = refs
    else:
        pad_ref, dest_ref, x_ref, _, xs_ref, zrow, sem = refs
    i = pl.program_id(0)

    if fill_pads:
        @pl.when(i == 0)
        def _fill():
            zrow[...] = jnp.zeros(zrow.shape, F32)
            for c in range(N_CLASSES):
                def start(r, carry):
                    _row_copy(zrow, 0, xs_ref, r, sem).start()
                    return carry

                def wait(r, carry):
                    _row_copy(zrow, 0, xs_ref, r, sem).wait()
                    return carry

                lax.fori_loop(pad_ref[0, c], pad_ref[1, c], start, 0)
                lax.fori_loop(pad_ref[0, c], pad_ref[1, c], wait, 0)

    for r in range(tbs):
        _row_copy(x_ref, r, xs_ref, dest_ref[0, r], sem).start(priority=r % 2)
    pltpu.make_async_copy(x_ref, xs_ref.at[pl.ds(0, tbs)], sem).wait()


def _scatter_call(pads, dest, x, xs_prev, n_sorted, tbs):
    n = x.shape[0]
    fill_pads = xs_prev is None
    in_specs = [pl.BlockSpec((None, 1, tbs), lambda i, pads: (i, 0, 0), memory_space=pltpu.SMEM),
                pl.BlockSpec((tbs, ROW_WIDTH), lambda i, pads: (i, 0))]
    args = [pads, dest, x]
    aliases = {}
    if not fill_pads:
        in_specs.append(pl.BlockSpec(memory_space=pl.ANY))
        args.append(xs_prev)
        aliases = {3: 0}
    return pl.pallas_call(
        functools.partial(_scatter_body, fill_pads, tbs),
        grid_spec=pltpu.PrefetchScalarGridSpec(
            num_scalar_prefetch=1,
            grid=(n // tbs,),
            in_specs=in_specs,
            out_specs=pl.BlockSpec(memory_space=pl.ANY),
            scratch_shapes=[pltpu.VMEM((8, ROW_WIDTH), F32), pltpu.SemaphoreType.DMA(())]),
        out_shape=jax.ShapeDtypeStruct((n_sorted, ROW_WIDTH), F32),
        input_output_aliases=aliases,
        compiler_params=pltpu.CompilerParams(
            dimension_semantics=("arbitrary",), vmem_limit_bytes=VMEM_LIMIT_BYTES),
        name="scatter_first" if fill_pads else "scatter_next",
    )(*args)


def _gather_body(tbs, dest_ref, ys_ref, y_ref, sem):
    for r in range(tbs):
        _row_copy(ys_ref, dest_ref[0, r], y_ref, r, sem).start(priority=r % 2)
    pltpu.make_async_copy(ys_ref.at[pl.ds(0, tbs)], y_ref, sem).wait()


def _gather_call(dest, ys, n, tbs):
    return pl.pallas_call(
        functools.partial(_gather_body, tbs),
        grid=(n // tbs,),
        in_specs=[pl.BlockSpec((None, 1, tbs), lambda i: (i, 0, 0), memory_space=pltpu.SMEM),
                  pl.BlockSpec(memory_space=pl.ANY)],
        out_specs=pl.BlockSpec((tbs, D_MODEL), lambda i: (i, 0)),
        out_shape=jax.ShapeDtypeStruct((n, D_MODEL), F32),
        scratch_shapes=[pltpu.SemaphoreType.DMA(())],
        compiler_params=pltpu.CompilerParams(
            dimension_semantics=("arbitrary",), vmem_limit_bytes=VMEM_LIMIT_BYTES),
        name="gather_rows",
    )(dest, ys)


def _moe_body(blk_ref, ea_ref, eb_ref, nused_ref, xs_ref, g2_ref,
              wga_ref, wua_ref, wda_ref, wgb_ref, wub_ref, wdb_ref, gf_ref, ys_ref):
    i = pl.program_id(0)

    @pl.when(i < nused_ref[0])
    def _tile():
        x1 = xs_ref[:, 0:D_MODEL]
        meta = xs_ref[:, D_MODEL:ROW_WIDTH]
        h2 = _rmsnorm(x1, g2_ref[...]).astype(BF16)

        def expert(wg_ref, wu_ref, wd_ref):
            a = jnp.dot(h2, wg_ref[...], preferred_element_type=F32)
            u = jnp.dot(h2, wu_ref[...], preferred_element_type=F32)
            act = ((a * _sigmoid(a)) * u).astype(BF16)
            return jnp.dot(act, wd_ref[...], preferred_element_type=F32)

        ya = expert(wga_ref, wua_ref, wda_ref)
        yb = expert(wgb_ref, wub_ref, wdb_ref)
        moe = meta[:, META_GA:META_GA + 1] * ya + meta[:, META_GB:META_GB + 1] * yb
        ys_ref[...] = _rmsnorm(x1 + moe, gf_ref[...])


def _moe_call(tile_blk, tile_ea, tile_eb, n_used, xs, g2, wg, wu, wd, gf):
    n_sorted = xs.shape[0]
    tm = MOE_TILE

    def const(shape):
        return pl.BlockSpec(shape, lambda i, blk, ea, eb, nu: (0,) * len(shape))

    def w_a(shape):
        return pl.BlockSpec((None,) + shape, lambda i, blk, ea, eb, nu: (ea[i], 0, 0))

    def w_b(shape):
        return pl.BlockSpec((None,) + shape, lambda i, blk, ea, eb, nu: (eb[i], 0, 0))

    return pl.pallas_call(
        _moe_body,
        grid_spec=pltpu.PrefetchScalarGridSpec(
            num_scalar_prefetch=4,
            grid=(n_sorted // tm,),
            in_specs=[pl.BlockSpec((tm, ROW_WIDTH), lambda i, blk, ea, eb, nu: (blk[i], 0)),
                      const((1, D_MODEL)),
                      w_a((D_MODEL, D_EXPERT)), w_a((D_MODEL, D_EXPERT)), w_a((D_EXPERT, D_MODEL)),
                      w_b((D_MODEL, D_EXPERT)), w_b((D_MODEL, D_EXPERT)), w_b((D_EXPERT, D_MODEL)),
                      const((1, D_MODEL))],
            out_specs=pl.BlockSpec((tm, D_MODEL), lambda i, blk, ea, eb, nu: (blk[i], 0))),
        out_shape=jax.ShapeDtypeStruct((n_sorted, D_MODEL), F32),
        compiler_params=pltpu.CompilerParams(
            dimension_semantics=("arbitrary",), vmem_limit_bytes=VMEM_LIMIT_BYTES),
        name="moe_sorted",
    )(tile_blk, tile_ea, tile_eb, n_used, xs, g2, wg, wu, wd, wg, wu, wd, gf)


def _pick_block(seq, target):
    tb = min(seq, target)
    assert seq % tb == 0 and tb % CHUNK == 0
    return tb


def _plan_tiles(counts, max_tiles):
    tiles_c = (counts + MOE_TILE - 1) // MOE_TILE
    cum_tiles = jnp.cumsum(tiles_c)
    offsets = (cum_tiles - tiles_c) * MOE_TILE
    n_used = cum_tiles[-1]
    tile_blk = jnp.minimum(jnp.arange(max_tiles, dtype=I32), n_used - 1)
    tile_cls = jnp.sum(tile_blk[:, None] >= cum_tiles[None, :], axis=1)
    tile_cls = jnp.minimum(tile_cls, N_CLASSES - 1).astype(I32)
    cls_ea = np.array([g * EXPERTS_PER_GROUP + a for g in range(N_GROUPS) for a, _ in PAIRS], np.int32)
    cls_eb = np.array([g * EXPERTS_PER_GROUP + b for g in range(N_GROUPS) for _, b in PAIRS], np.int32)
    pads = jnp.stack([offsets + counts, offsets + tiles_c * MOE_TILE]).astype(I32)
    return (offsets.astype(I32), tile_blk, _lookup(cls_ea, tile_cls), _lookup(cls_eb, tile_cls),
            n_used.reshape(1).astype(I32), pads)


def _lookup(table, idx):
    table = jnp.asarray(table, I32)
    k = jnp.arange(table.shape[0], dtype=I32)
    return jnp.sum(jnp.where(idx[..., None] == k, table, 0), axis=-1).astype(I32)


def kernel(x_prompt, x_sample, cache_k, cache_v, state_hgrn, norm1_g, w_in, attn_sink, lower_bounds,
           hgrn_norm_g, w_o, norm2_g, w_router_group, b_router_group, w_router_expert, b_router_expert,
           w_gate, w_up, w_down, final_norm_g):
    depth = w_in.shape[0]
    assert depth == 1
    l = 0
    w_hist = cache_k.shape[2]
    assert w_hist == WINDOW
    w_r = jnp.concatenate(
        [w_router_group[l], jnp.transpose(w_router_expert[l], (1, 0, 2)).reshape(D_MODEL, N_EXPERTS)], axis=1)
    w_r = jnp.pad(w_r, ((0, 0), (0, LANES - w_r.shape[1]))).astype(BF16)
    b_r = jnp.concatenate([b_router_group[l], b_router_expert[l].reshape(N_EXPERTS)])
    b_r = jnp.pad(b_r, (0, LANES - b_r.shape[0])).reshape(1, LANES).astype(F32)
    weights = (norm1_g[l].reshape(1, D_MODEL), w_in[l].astype(BF16), attn_sink[l].astype(F32),
               lower_bounds.astype(F32), hgrn_norm_g[l].reshape(1, B_WIDTH), w_o[l].astype(BF16),
               norm2_g[l].reshape(1, D_MODEL), w_r, b_r)
    g2 = norm2_g[l].reshape(1, D_MODEL)
    gf = final_norm_g.reshape(1, D_MODEL)
    wg, wu, wd = w_gate[l].astype(BF16), w_up[l].astype(BF16), w_down[l].astype(BF16)

    bp, lp, _ = x_prompt.shape
    bs, ls, _ = x_sample.shape
    np_tok, ns_tok = bp * lp, bs * ls
    tbp, tbs = _pick_block(lp, LAYER_BLOCK), _pick_block(ls, LAYER_BLOCK)
    zero_cnt = jnp.zeros((CLS_ROWS, LANES), F32)
    xrp, clsp, rankp, cntp, kp, vp, sp = _layer_call(x_prompt, None, zero_cnt, weights, 0, tbp)
    cache = (cache_k[l].reshape(bs, w_hist, A_KV_WIDTH), cache_v[l].reshape(bs, w_hist, A_KV_WIDTH), state_hgrn[l])
    xrs, clss, ranks, cnt, kn, vn, sn = _layer_call(x_sample, cache, cntp, weights, PAST_LEN, tbs)

    max_tiles = (np_tok + ns_tok) // MOE_TILE + N_CLASSES
    counts = cnt[:N_CLASSES, 0].astype(I32)
    offsets, tile_blk, tile_ea, tile_eb, n_used, pads = _plan_tiles(counts, max_tiles)
    destp = _lookup(offsets, clsp) + rankp
    dests = _lookup(offsets, clss) + ranks

    n_sorted = max_tiles * MOE_TILE
    xs = _scatter_call(pads, destp, xrp.reshape(np_tok, ROW_WIDTH), None, n_sorted, tbp)
    xs = _scatter_call(pads, dests, xrs.reshape(ns_tok, ROW_WIDTH), xs, n_sorted, tbs)
    ysort = _moe_call(tile_blk, tile_ea, tile_eb, n_used, xs, g2, wg, wu, wd, gf)
    yp = _gather_call(destp, ysort, np_tok, tbp)
    ys = _gather_call(dests, ysort, ns_tok, tbs)

    kv_shape = (1, -1, w_hist, A_KV_HEADS, A_HEAD_DIM)
    return (yp.reshape(bp, lp, D_MODEL), ys.reshape(bs, ls, D_MODEL),
            kp.reshape(kv_shape), vp.reshape(kv_shape), sp[None],
            kn.reshape(kv_shape), vn.reshape(kv_shape), sn[None])
```

```python
import functools

import numpy as np
import jax
import jax.numpy as jnp
from jax import lax
from jax.experimental import pallas as pl
from jax.experimental.pallas import tpu as pltpu

F32 = jnp.float32
BF16 = jnp.bfloat16
I32 = jnp.int32

D_MODEL = 1024
CHUNK = 64
EPS = 1e-6
PAST_LEN = 4096
WINDOW = 128
A_HEADS = 8
A_KV_HEADS = 2
A_HEAD_DIM = 64
A_GROUP = A_HEADS // A_KV_HEADS
A_WIDTH = A_HEADS * A_HEAD_DIM
A_KV_WIDTH = A_KV_HEADS * A_HEAD_DIM
B_HEADS = 4
B_KEY_DIM = 128
B_VAL_DIM = 128
B_KEY_WIDTH = B_HEADS * B_KEY_DIM
B_WIDTH = B_HEADS * B_VAL_DIM
MIX_WIDTH = A_WIDTH + B_WIDTH
OFF_K = A_WIDTH
OFF_V = OFF_K + A_KV_WIDTH
OFF_QB = OFF_V + A_KV_WIDTH
OFF_FB = OFF_QB + B_KEY_WIDTH
OFF_IB = OFF_FB + B_KEY_WIDTH
OFF_GB = OFF_IB + B_WIDTH
N_IN = OFF_GB + B_WIDTH
N_GROUPS = 4
EXPERTS_PER_GROUP = 4
N_EXPERTS = N_GROUPS * EXPERTS_PER_GROUP
D_EXPERT = 256
LANES = 128
SUBLANES = 8
ROUTE_E0 = N_GROUPS
PAIRS = [(a, b) for a in range(EXPERTS_PER_GROUP) for b in range(a + 1, EXPERTS_PER_GROUP)]
N_PAIRS = len(PAIRS)
N_CLASSES = N_GROUPS * N_PAIRS
CLS_ROWS = 32
ROW_WIDTH = D_MODEL + LANES
META_GA, META_GB, META_TOK_HI, META_TOK_LO = 0, 1, 2, 3
TOK_SPLIT = 256
LAYER_BLOCK = 512
CHUNK_UNROLL = 4
MOE_TILE = 256
MOE_TILE_SHIFT = 8
TILE_LANES = 256
TAB_CLS, TAB_ORD, TAB_USED = 0, 1, 2

VMEM_LIMIT_BYTES = 56 * 1024 * 1024

NT_DIMS = (((1,), (1,)), ((), ()))
TN_DIMS = (((0,), (0,)), ((), ()))


def _rmsnorm(x, g):
    return x * lax.rsqrt(jnp.mean(x * x, axis=-1, keepdims=True) + EPS) * g


def _silu(x):
    hx = 0.5 * x
    return hx + hx * jnp.tanh(hx)


def _route(logits):
    lane = lax.broadcasted_iota(I32, logits.shape, 1)
    neg = -jnp.inf
    is_g = lane < N_GROUPS
    gl = jnp.where(is_g, logits, neg)
    gmax = jnp.max(gl, axis=-1, keepdims=True)
    gidx = jnp.min(jnp.where(gl == gmax, lane, LANES), axis=-1, keepdims=True)
    p_group = 1.0 / jnp.sum(jnp.where(is_g, jnp.exp(logits - gmax), 0.0), axis=-1, keepdims=True)
    e_lo = ROUTE_E0 + EXPERTS_PER_GROUP * gidx
    in_group = (lane >= e_lo) & (lane < e_lo + EXPERTS_PER_GROUP)
    el = jnp.where(in_group, logits, neg)
    e1 = jnp.max(el, axis=-1, keepdims=True)
    i1 = jnp.min(jnp.where(el == e1, lane, LANES), axis=-1, keepdims=True)
    el2 = jnp.where(lane == i1, neg, el)
    e2 = jnp.max(el2, axis=-1, keepdims=True)
    i2 = jnp.min(jnp.where(el2 == e2, lane, LANES), axis=-1, keepdims=True)
    t = jnp.exp(e2 - e1)
    w1 = p_group / (1.0 + t)
    w2 = w1 * t
    first_low = i1 < i2
    ea = jnp.where(first_low, i1, i2) - e_lo
    eb = jnp.where(first_low, i2, i1) - e_lo
    ga = jnp.where(first_low, w1, w2)
    gb = jnp.where(first_low, w2, w1)
    pair = ((ea * (2 * EXPERTS_PER_GROUP - 1 - ea)) >> 1) + eb - ea - 1
    dense = jnp.where(lane == i1, w1, 0.0) + jnp.where(lane == i2, w2, 0.0)
    return gidx * N_PAIRS + pair, ga, gb, dense


def _mixer_and_router(pos0, tb, j, x_ref, g1_ref, win_ref, sink_ref, lb_ref, hg_ref, wo_ref, g2_ref, wr_ref, br_ref,
                      proj_s, kbuf, vbuf, st_s, mix_s):
    nc = tb // CHUNK
    nk = WINDOW + CHUNK
    x = x_ref[...]
    h = _rmsnorm(x, g1_ref[...]).astype(BF16)
    proj_s[...] = jnp.dot(h, win_ref[...], preferred_element_type=F32)
    kbuf[WINDOW:WINDOW + tb, :] = proj_s[:, OFF_K:OFF_K + A_KV_WIDTH]
    vbuf[WINDOW:WINDOW + tb, :] = proj_s[:, OFF_V:OFF_V + A_KV_WIDTH]

    row = lax.broadcasted_iota(I32, (A_GROUP * CHUNK, nk), 0)
    col = lax.broadcasted_iota(I32, (A_GROUP * CHUNK, nk), 1)
    dist = jnp.abs((row & (CHUNK - 1)) - (col - WINDOW)).astype(F32)
    row_head = row // CHUNK
    rowc_head = lax.broadcasted_iota(I32, (A_GROUP * CHUNK, 1), 0) // CHUNK
    bias = []
    sinkc = []
    for hk in range(A_KV_HEADS):
        slope = jnp.exp2(-(row_head + (hk * A_GROUP + 1)).astype(F32))
        bias.append(slope * dist)
        sc = jnp.zeros((A_GROUP * CHUNK, 1), F32)
        for g in range(A_GROUP):
            sc = jnp.where(rowc_head == g, sink_ref[hk * A_GROUP + g], sc)
        sinkc.append(sc)

    lbr = lb_ref[...]
    lbm = jnp.max(lbr, axis=0, keepdims=True)
    lbe = jnp.exp(lbr - lbm)
    lb = lbe[0:1, :] / jnp.sum(lbe, axis=0, keepdims=True)
    f_mid = 0.5 * (1.0 + lb)
    f_half = 0.5 * (1.0 - lb)
    hg = hg_ref[...]
    tr = lax.broadcasted_iota(I32, (CHUNK, CHUNK), 0)
    tc = lax.broadcasted_iota(I32, (CHUNK, CHUNK), 1)
    tril = tr >= tc
    ltri = jnp.where(tril, 1.0, 0.0).astype(BF16)

    def chunk_body(c, carry):
        r0 = pl.multiple_of(c * CHUNK, CHUNK)
        rows = pl.ds(r0, CHUNK)
        valid = col >= (WINDOW - pos0) - (j * tb + c * CHUNK)
        for hk in range(A_KV_HEADS):
            qc = proj_s[rows, hk * A_GROUP * A_HEAD_DIM:(hk + 1) * A_GROUP * A_HEAD_DIM] * (A_HEAD_DIM ** -0.5)
            q4 = jnp.concatenate([qc[:, g * A_HEAD_DIM:(g + 1) * A_HEAD_DIM] for g in range(A_GROUP)],
                                 axis=0).astype(BF16)
            kw = kbuf[pl.ds(r0, nk), hk * A_HEAD_DIM:(hk + 1) * A_HEAD_DIM].astype(BF16)
            vw = vbuf[pl.ds(r0, nk), hk * A_HEAD_DIM:(hk + 1) * A_HEAD_DIM].astype(BF16)
            s = lax.dot_general(q4, kw, NT_DIMS, preferred_element_type=F32) - bias[hk]
            s = jnp.where(valid, s, -jnp.inf)
            m = jnp.maximum(jnp.max(s, axis=-1, keepdims=True), sinkc[hk])
            p = jnp.exp(s - m)
            den = jnp.sum(p, axis=-1, keepdims=True) + jnp.exp(sinkc[hk] - m)
            o = jnp.dot(p.astype(BF16), vw, preferred_element_type=F32) / den
            att = jnp.concatenate([o[g * CHUNK:(g + 1) * CHUNK] for g in range(A_GROUP)], axis=1)
            mix_s[rows, hk * A_GROUP * A_HEAD_DIM:(hk + 1) * A_GROUP * A_HEAD_DIM] = att.astype(BF16)

        qb = proj_s[rows, OFF_QB:OFF_QB + B_KEY_WIDTH]
        fl = proj_s[rows, OFF_FB:OFF_FB + B_KEY_WIDTH]
        vb = proj_s[rows, OFF_IB:OFF_IB + B_WIDTH].astype(BF16)
        gg = proj_s[rows, OFF_GB:OFF_GB + B_WIDTH]
        f = f_mid + f_half * jnp.tanh(0.5 * fl)
        logf = jnp.log(f)
        hi = logf.astype(BF16)
        lo = (logf - hi.astype(F32)).astype(BF16)
        cum = (jnp.dot(ltri, hi, preferred_element_type=F32) + jnp.dot(ltri, lo, preferred_element_type=F32))
        ref = cum[CHUNK // 2:CHUNK // 2 + 1, :]
        tot = cum[CHUNK - 1:CHUNK, :]
        qf = _silu(qb)
        kf = 1.0 - f
        e_fwd = jnp.exp(cum - ref)
        e_bwd = 1.0 / e_fwd
        q_rel = qf * e_fwd
        k_rel = kf * e_bwd
        q_in = q_rel.astype(BF16)
        k_in = k_rel.astype(BF16)
        k_tot = (k_rel * jnp.exp(tot - ref)).astype(BF16)
        q_cum = (q_rel * jnp.exp(ref)).astype(BF16)
        dec = jnp.exp(tot)
        outs = []
        for hh in range(B_HEADS):
            sl = slice(hh * B_KEY_DIM, (hh + 1) * B_KEY_DIM)
            a = lax.dot_general(q_in[:, sl], k_in[:, sl], NT_DIMS, preferred_element_type=F32)
            a = jnp.where(tril, a, 0.0).astype(BF16)
            st = st_s[hh]
            o = (jnp.dot(a, vb[:, sl], preferred_element_type=F32)
                 + lax.dot_general(q_cum[:, sl], st.astype(BF16), NT_DIMS, preferred_element_type=F32))
            ds_t = lax.dot_general(vb[:, sl], k_tot[:, sl], TN_DIMS, preferred_element_type=F32)
            st_s[hh] = st * dec[:, sl] + ds_t
            o = o * lax.rsqrt(jnp.mean(o * o, axis=-1, keepdims=True) + EPS) * hg[:, sl]
            outs.append(o)
        rec = jnp.concatenate(outs, axis=1) * _silu(gg)
        mix_s[rows, A_WIDTH:A_WIDTH + B_WIDTH] = rec.astype(BF16)
        return carry

    lax.fori_loop(0, nc, chunk_body, 0, unroll=min(nc, CHUNK_UNROLL))

    x1 = x + jnp.dot(mix_s[...], wo_ref[...], preferred_element_type=F32)
    h2 = _rmsnorm(x1, g2_ref[...]).astype(BF16)
    logits = jnp.dot(h2, wr_ref[...], preferred_element_type=F32) + br_ref[...]
    return x1, logits


def _row_copy(src_ref, src_row, dst_ref, dst_row, sem):
    return pltpu.make_async_copy(src_ref.at[pl.ds(src_row, 1)], dst_ref.at[pl.ds(dst_row, 1)], sem)


def _issue_rows(n, src_ref, src_slot, dst_ref, idx_ref, sem):
    for r in range(n):
        _row_copy(src_ref.at[src_slot], r, dst_ref, idx_ref[src_slot, r], sem.at[src_slot]).start(priority=r % 2)


def _layer_body(pos0, tb, sorted_out, *refs):
    if sorted_out:
        (x_ref, g1_ref, win_ref, sink_ref, lb_ref, hg_ref, wo_ref, g2_ref, wr_ref, br_ref,
         xs_ref, tab_ref, cnt_ref, kwin_ref, vwin_ref, sout_ref,
         proj_s, kbuf, vbuf, st_s, mix_s, xrow_s, cnt_s, cur_s, tab_s, dest_v, dest_sm, dsem, rsem) = refs
    else:
        (x_ref, ck_ref, cv_ref, s0_ref, g1_ref, win_ref, sink_ref, lb_ref, hg_ref, wo_ref, g2_ref, wr_ref, br_ref,
         x1_ref, route_ref, kwin_ref, vwin_ref, sout_ref,
         proj_s, kbuf, vbuf, st_s, mix_s) = refs
    b = pl.program_id(0)
    j = pl.program_id(1)
    nblk = pl.num_programs(1)
    step = b * nblk + j

    if sorted_out:
        slot = step % 2
        prev = 1 - slot
        n_tiles = xs_ref.shape[0] // MOE_TILE
        spare_row0 = (n_tiles - tb // MOE_TILE) * MOE_TILE

        @pl.when(step == 0)
        def _init_routing():
            cnt_s[...] = jnp.zeros(cnt_s.shape, F32)
            cur_s[...] = jnp.zeros(cur_s.shape, F32)
            tab_s[...] = jnp.zeros(tab_s.shape, I32)
            xrow_s[1] = jnp.zeros(xrow_s.shape[1:], F32)

            def fill(r, carry):
                dest_sm[1, r] = spare_row0 + r
                return carry

            lax.fori_loop(0, tb, fill, 0)

    @pl.when(j == 0)
    def _init_stream():
        if sorted_out:
            kbuf[0:WINDOW, :] = jnp.zeros((WINDOW, A_KV_WIDTH), F32)
            vbuf[0:WINDOW, :] = jnp.zeros((WINDOW, A_KV_WIDTH), F32)
            st_s[...] = jnp.zeros(st_s.shape, F32)
        else:
            kbuf[0:WINDOW, :] = ck_ref[...]
            vbuf[0:WINDOW, :] = cv_ref[...]
            for hh in range(B_HEADS):
                st_s[hh] = s0_ref[hh].T

    if sorted_out:
        _issue_rows(tb, xrow_s, prev, xs_ref, dest_sm, rsem)

    x1, logits = _mixer_and_router(pos0, tb, j, x_ref, g1_ref, win_ref, sink_ref, lb_ref, hg_ref, wo_ref, g2_ref,
                                   wr_ref, br_ref, proj_s, kbuf, vbuf, st_s, mix_s)
    cls, ga, gb, dense = _route(logits)

    if not sorted_out:
        x1_ref[...] = x1
        route_ref[...] = dense
    else:
        lane = lax.broadcasted_iota(I32, (tb, LANES), 1)
        tok = step * tb + lax.broadcasted_iota(I32, (tb, 1), 0)
        tok_hi = (tok >> 8).astype(F32)
        tok_lo = (tok & (TOK_SPLIT - 1)).astype(F32)
        meta = jnp.where(lane == META_GA, ga,
                         jnp.where(lane == META_GB, gb,
                                   jnp.where(lane == META_TOK_HI, tok_hi,
                                             jnp.where(lane == META_TOK_LO, tok_lo, 0.0))))
        xrow_s[slot, :, 0:D_MODEL] = x1
        xrow_s[slot, :, D_MODEL:ROW_WIDTH] = meta

        onehot = jnp.where(lane == cls, 1.0, 0.0).astype(BF16)
        er = lax.broadcasted_iota(I32, (CLS_ROWS, LANES), 0)
        ec = lax.broadcasted_iota(I32, (CLS_ROWS, LANES), 1)
        eye = jnp.where(er == ec, 1.0, 0.0).astype(BF16)
        oht = lax.dot_general(eye, onehot, NT_DIMS, preferred_element_type=F32)
        ur = lax.broadcasted_iota(I32, (tb, tb), 0)
        uc = lax.broadcasted_iota(I32, (tb, tb), 1)
        before = jnp.where(ur < uc, 1.0, 0.0).astype(BF16)
        prefix = jnp.dot(oht.astype(BF16), before, preferred_element_type=F32)
        cnt = cnt_s[:, 0:1]
        rank = jnp.sum(oht * (prefix + cnt), axis=0, keepdims=True).astype(I32)

        n_c = jnp.sum(oht, axis=1, keepdims=True)
        cnt_i = cnt.astype(I32)
        after_i = (cnt + n_c).astype(I32)
        tiles_before = (cnt_i + (MOE_TILE - 1)) >> MOE_TILE_SHIFT
        new_c = ((after_i + (MOE_TILE - 1)) >> MOE_TILE_SHIFT) - tiles_before
        sr = lax.broadcasted_iota(I32, (CLS_ROWS, CLS_ROWS), 0)
        sc = lax.broadcasted_iota(I32, (CLS_ROWS, CLS_ROWS), 1)
        lower = jnp.where(sr > sc, 1.0, 0.0).astype(BF16)
        new_b = jnp.broadcast_to(new_c.astype(F32), (CLS_ROWS, LANES)).astype(BF16)
        opened_before = jnp.dot(lower, new_b, preferred_element_type=F32)[:, 0:1].astype(I32)
        used = tab_s[TAB_USED:TAB_USED + 1, 0:1]
        base_c = used + opened_before
        partial = (cnt_i & (MOE_TILE - 1)) != 0
        cur_c = cur_s[:, 0:1].astype(I32)
        first_c = jnp.where(partial, cur_c, base_c)
        shift_c = jnp.where(partial, base_c - 1, base_c)
        ord0_c = cnt_i >> MOE_TILE_SHIFT

        def per_token(col):
            return jnp.sum(oht * col.astype(F32), axis=0, keepdims=True).astype(I32)

        ord_t = rank >> MOE_TILE_SHIFT
        d_t = ord_t - per_token(ord0_c)
        tile_t = jnp.where(d_t == 0, per_token(first_c), per_token(shift_c) + d_t)
        dest = tile_t * MOE_TILE + (rank & (MOE_TILE - 1))

        tl = lax.broadcasted_iota(I32, (CLS_ROWS, TILE_LANES), 1)
        opened = (tl >= base_c) & (tl < base_c + new_c)
        cls_id = lax.broadcasted_iota(I32, (CLS_ROWS, TILE_LANES), 0)
        hit = jnp.sum(jnp.where(opened, 1, 0), axis=0, keepdims=True) > 0
        t_cls = jnp.sum(jnp.where(opened, cls_id, 0), axis=0, keepdims=True)
        t_ord = jnp.sum(jnp.where(opened, tiles_before + (tl - base_c), 0), axis=0, keepdims=True)
        tab_s[TAB_CLS:TAB_CLS + 1, :] = jnp.where(hit, t_cls, tab_s[TAB_CLS:TAB_CLS + 1, :])
        tab_s[TAB_ORD:TAB_ORD + 1, :] = jnp.where(hit, t_ord, tab_s[TAB_ORD:TAB_ORD + 1, :])
        used_new = used + jnp.sum(new_c, axis=0, keepdims=True)
        tab_s[TAB_USED:TAB_USED + 1, :] = jnp.broadcast_to(used_new, (1, TILE_LANES))
        cur_s[...] = jnp.broadcast_to(jnp.where(new_c > 0, base_c + new_c - 1, cur_c).astype(F32), cur_s.shape)
        cnt_s[...] = jnp.broadcast_to(cnt + n_c, cnt_s.shape)
        tab_ref[...] = tab_s[...]
        cnt_ref[...] = cnt_s[...]

        dest_v[...] = jnp.broadcast_to(dest, dest_v.shape)
        to_smem = pltpu.make_async_copy(dest_v.at[pl.ds(0, 1)], dest_sm.at[pl.ds(slot, 1)], dsem)
        to_smem.start()
        to_smem.wait()

        def wait_rows(s):
            pltpu.make_async_copy(xrow_s.at[s], xs_ref.at[pl.ds(0, tb)], rsem.at[s]).wait()

        wait_rows(prev)

        @pl.when(step == pl.num_programs(0) * nblk - 1)
        def _flush():
            _issue_rows(tb, xrow_s, slot, xs_ref, dest_sm, rsem)
            wait_rows(slot)

    kt = kbuf[tb:tb + WINDOW, :]
    vt = vbuf[tb:tb + WINDOW, :]
    kbuf[0:WINDOW, :] = kt
    vbuf[0:WINDOW, :] = vt
    kwin_ref[...] = kt
    vwin_ref[...] = vt
    for hh in range(B_HEADS):
        sout_ref[hh] = st_s[hh].T


def _layer_call(x, cache, weights, pos0, tb, n_sorted_tiles=None):
    bsz, seq, _ = x.shape
    nblk = seq // tb
    sorted_out = cache is None
    g1, w_in, sink, lower_bounds, hg, w_o, g2, w_r, b_r = weights

    def const(shape):
        return pl.BlockSpec(shape, lambda b, j: (0,) * len(shape))

    def per_stream(shape):
        return pl.BlockSpec((None,) + shape, lambda b, j: (b,) + (0,) * len(shape))

    in_specs = [pl.BlockSpec((None, tb, D_MODEL), lambda b, j: (b, j, 0))]
    args = [x]
    if not sorted_out:
        in_specs += [per_stream((WINDOW, A_KV_WIDTH)), per_stream((WINDOW, A_KV_WIDTH)),
                     per_stream((B_HEADS, B_KEY_DIM, B_VAL_DIM))]
        args += list(cache)
    in_specs += [const((1, D_MODEL)), const((D_MODEL, N_IN)),
                 pl.BlockSpec(memory_space=pltpu.SMEM),
                 const((2, B_KEY_WIDTH)), const((1, B_WIDTH)), const((MIX_WIDTH, D_MODEL)),
                 const((1, D_MODEL)), const((D_MODEL, LANES)), const((1, LANES))]
    args += [g1, w_in, sink, lower_bounds, hg, w_o, g2, w_r, b_r]
    stream_shapes = (jax.ShapeDtypeStruct((bsz, WINDOW, A_KV_WIDTH), F32),
                     jax.ShapeDtypeStruct((bsz, WINDOW, A_KV_WIDTH), F32),
                     jax.ShapeDtypeStruct((bsz, B_HEADS, B_KEY_DIM, B_VAL_DIM), F32))
    stream_specs = (per_stream((WINDOW, A_KV_WIDTH)), per_stream((WINDOW, A_KV_WIDTH)),
                    per_stream((B_HEADS, B_KEY_DIM, B_VAL_DIM)))
    scratch = [pltpu.VMEM((tb, N_IN), F32),
               pltpu.VMEM((WINDOW + tb, A_KV_WIDTH), F32),
               pltpu.VMEM((WINDOW + tb, A_KV_WIDTH), F32),
               pltpu.VMEM((B_HEADS, B_VAL_DIM, B_KEY_DIM), F32),
               pltpu.VMEM((tb, MIX_WIDTH), BF16)]
    if sorted_out:
        out_shape = (jax.ShapeDtypeStruct((n_sorted_tiles * MOE_TILE, ROW_WIDTH), F32),
                     jax.ShapeDtypeStruct((SUBLANES, TILE_LANES), I32),
                     jax.ShapeDtypeStruct((CLS_ROWS, LANES), F32)) + stream_shapes
        out_specs = (pl.BlockSpec(memory_space=pl.ANY), const((SUBLANES, TILE_LANES)),
                     const((CLS_ROWS, LANES))) + stream_specs
        scratch += [pltpu.VMEM((2, tb, ROW_WIDTH), F32),
                    pltpu.VMEM((CLS_ROWS, LANES), F32),
                    pltpu.VMEM((CLS_ROWS, LANES), F32),
                    pltpu.VMEM((SUBLANES, TILE_LANES), I32),
                    pltpu.VMEM((SUBLANES, tb), I32),
                    pltpu.SMEM((2, tb), I32),
                    pltpu.SemaphoreType.DMA(()),
                    pltpu.SemaphoreType.DMA((2,))]
    else:
        out_shape = (jax.ShapeDtypeStruct((bsz, seq, D_MODEL), F32),
                     jax.ShapeDtypeStruct((bsz, seq, LANES), F32)) + stream_shapes
        out_specs = (pl.BlockSpec((None, tb, D_MODEL), lambda b, j: (b, j, 0)),
                     pl.BlockSpec((None, tb, LANES), lambda b, j: (b, j, 0))) + stream_specs
    return pl.pallas_call(
        functools.partial(_layer_body, pos0, tb, sorted_out),
        grid=(bsz, nblk),
        in_specs=in_specs,
        out_specs=out_specs,
        out_shape=out_shape,
        scratch_shapes=scratch,
        compiler_params=pltpu.CompilerParams(
            dimension_semantics=("arbitrary", "arbitrary"),
            vmem_limit_bytes=VMEM_LIMIT_BYTES),
        name="layer_prompt" if sorted_out else "layer_sample",
    )(*args)


def _moe_sorted_body(blk_ref, ea_ref, eb_ref, nv_ref, nused_ref, xs_ref, g2_ref,
                     wga_ref, wua_ref, wda_ref, wgb_ref, wub_ref, wdb_ref, gf_ref, y_ref,
                     ybuf, tok_v, tok_sm, tsem, rsem):
    p = pl.program_id(0)
    n_used = nused_ref[0]
    slot = p % 2
    prev = 1 - slot
    nv_prev = nv_ref[jnp.maximum(p - 1, 0)]

    def compute():
        x1 = xs_ref[:, 0:D_MODEL]
        meta = xs_ref[:, D_MODEL:ROW_WIDTH]
        h2 = _rmsnorm(x1, g2_ref[...]).astype(BF16)

        def expert(wg_ref, wu_ref, wd_ref):
            a = jnp.dot(h2, wg_ref[...], preferred_element_type=F32)
            u = jnp.dot(h2, wu_ref[...], preferred_element_type=F32)
            return jnp.dot((_silu(a) * u).astype(BF16), wd_ref[...], preferred_element_type=F32)

        ya = expert(wga_ref, wua_ref, wda_ref)
        yb = expert(wgb_ref, wub_ref, wdb_ref)
        moe = meta[:, META_GA:META_GA + 1] * ya + meta[:, META_GB:META_GB + 1] * yb
        ybuf[slot] = _rmsnorm(x1 + moe, gf_ref[...])
        sr = lax.broadcasted_iota(I32, (SUBLANES, LANES), 0)
        sc = lax.broadcasted_iota(I32, (SUBLANES, LANES), 1)
        pick = jnp.where(sc == sr + META_TOK_HI, 1.0, 0.0).astype(BF16)
        ids = lax.dot_general(pick, meta.astype(BF16), NT_DIMS, preferred_element_type=F32)
        tok = (ids[0:1, :] * TOK_SPLIT + ids[1:2, :]).astype(I32)
        tok_v[...] = jnp.broadcast_to(tok, tok_v.shape)
        to_smem = pltpu.make_async_copy(tok_v.at[pl.ds(0, 1)], tok_sm.at[pl.ds(slot, 1)], tsem)
        to_smem.start()
        to_smem.wait()

    def send_full():
        _issue_rows(MOE_TILE, ybuf, prev, y_ref, tok_sm, rsem)

    def wait_full():
        pltpu.make_async_copy(ybuf.at[prev], y_ref.at[pl.ds(0, MOE_TILE)], rsem.at[prev]).wait()

    def send_part():
        def start(r, carry):
            _row_copy(ybuf.at[prev], r, y_ref, tok_sm[prev, r], rsem.at[prev]).start()
            return carry

        lax.fori_loop(0, nv_prev, start, 0)

    def wait_part():
        def wait(r, carry):
            _row_copy(ybuf.at[prev], r, y_ref, tok_sm[prev, r], rsem.at[prev]).wait()
            return carry

        lax.fori_loop(0, nv_prev, wait, 0)

    has_prev = (p >= 1) & (p <= n_used)
    prev_full = nv_prev == MOE_TILE
    active = p < n_used

    @pl.when(active & jnp.logical_not(has_prev))
    def _first():
        compute()

    @pl.when(active & has_prev & prev_full)
    def _steady():
        send_full()
        compute()
        wait_full()

    @pl.when(active & has_prev & jnp.logical_not(prev_full))
    def _after_partial():
        send_part()
        compute()
        wait_part()

    @pl.when(jnp.logical_not(active) & has_prev & prev_full)
    def _flush_full():
        send_full()
        wait_full()

    @pl.when(jnp.logical_not(active) & has_prev & jnp.logical_not(prev_full))
    def _flush_part():
        send_part()
        wait_part()


def _moe_sorted_call(step_blk, step_ea, step_eb, step_nv, n_used, xs, g2, wg, wu, wd, gf, n_tok):
    n_steps = step_blk.shape[0]

    def const(shape):
        return pl.BlockSpec(shape, lambda p, blk, ea, eb, nv, nu: (0,) * len(shape))

    def w_a(shape):
        return pl.BlockSpec((None,) + shape, lambda p, blk, ea, eb, nv, nu: (ea[p], 0, 0))

    def w_b(shape):
        return pl.BlockSpec((None,) + shape, lambda p, blk, ea, eb, nv, nu: (eb[p], 0, 0))

    return pl.pallas_call(
        _moe_sorted_body,
        grid_spec=pltpu.PrefetchScalarGridSpec(
            num_scalar_prefetch=5,
            grid=(n_steps,),
            in_specs=[pl.BlockSpec((MOE_TILE, ROW_WIDTH), lambda p, blk, ea, eb, nv, nu: (blk[p], 0)),
                      const((1, D_MODEL)),
                      w_a((D_MODEL, D_EXPERT)), w_a((D_MODEL, D_EXPERT)), w_a((D_EXPERT, D_MODEL)),
                      w_b((D_MODEL, D_EXPERT)), w_b((D_MODEL, D_EXPERT)), w_b((D_EXPERT, D_MODEL)),
                      const((1, D_MODEL))],
            out_specs=pl.BlockSpec(memory_space=pl.ANY),
            scratch_shapes=[pltpu.VMEM((2, MOE_TILE, D_MODEL), F32),
                            pltpu.VMEM((SUBLANES, MOE_TILE), I32),
                            pltpu.SMEM((2, MOE_TILE), I32),
                            pltpu.SemaphoreType.DMA(()),
                            pltpu.SemaphoreType.DMA((2,))]),
        out_shape=jax.ShapeDtypeStruct((n_tok, D_MODEL), F32),
        compiler_params=pltpu.CompilerParams(
            dimension_semantics=("arbitrary",), vmem_limit_bytes=VMEM_LIMIT_BYTES),
        name="moe_sorted",
    )(step_blk, step_ea, step_eb, step_nv, n_used, xs, g2, wg, wu, wd, wg, wu, wd, gf)


def _moe_dense_body(x1_ref, route_ref, g2_ref, wg_ref, wu_ref, wd_ref, gf_ref, y_ref, h2_s, acc_s):
    e = pl.program_id(1)

    @pl.when(e == 0)
    def _first():
        h2_s[...] = _rmsnorm(x1_ref[...], g2_ref[...]).astype(BF16)
        acc_s[...] = jnp.zeros(acc_s.shape, F32)

    h2 = h2_s[...]
    a = jnp.dot(h2, wg_ref[...], preferred_element_type=F32)
    u = jnp.dot(h2, wu_ref[...], preferred_element_type=F32)
    y = jnp.dot((_silu(a) * u).astype(BF16), wd_ref[...], preferred_element_type=F32)
    route = route_ref[...]
    lane = lax.broadcasted_iota(I32, route.shape, 1)
    gate = jnp.sum(jnp.where(lane == e + ROUTE_E0, route, 0.0), axis=-1, keepdims=True)
    acc_s[...] += gate * y

    @pl.when(e == N_EXPERTS - 1)
    def _last():
        y_ref[...] = _rmsnorm(x1_ref[...] + acc_s[...], gf_ref[...])


def _moe_dense_call(x1, route, g2, wg, wu, wd, gf, tm):
    n = x1.shape[0]
    return pl.pallas_call(
        _moe_dense_body,
        grid=(n // tm, N_EXPERTS),
        in_specs=[pl.BlockSpec((tm, D_MODEL), lambda i, e: (i, 0)),
                  pl.BlockSpec((tm, LANES), lambda i, e: (i, 0)),
                  pl.BlockSpec((1, D_MODEL), lambda i, e: (0, 0)),
                  pl.BlockSpec((None, D_MODEL, D_EXPERT), lambda i, e: (e, 0, 0)),
                  pl.BlockSpec((None, D_MODEL, D_EXPERT), lambda i, e: (e, 0, 0)),
                  pl.BlockSpec((None, D_EXPERT, D_MODEL), lambda i, e: (e, 0, 0)),
                  pl.BlockSpec((1, D_MODEL), lambda i, e: (0, 0))],
        out_specs=pl.BlockSpec((tm, D_MODEL), lambda i, e: (i, 0)),
        out_shape=jax.ShapeDtypeStruct((n, D_MODEL), F32),
        scratch_shapes=[pltpu.VMEM((tm, D_MODEL), BF16), pltpu.VMEM((tm, D_MODEL), F32)],
        compiler_params=pltpu.CompilerParams(
            dimension_semantics=("arbitrary", "arbitrary"),
            vmem_limit_bytes=VMEM_LIMIT_BYTES),
        name="moe_dense",
    )(x1, route, g2, wg, wu, wd, gf)


def _pick_block(seq, target):
    tb = min(seq, target)
    assert seq % tb == 0 and tb % CHUNK == 0
    return tb


def _lookup(table, idx):
    table = jnp.asarray(table, I32)
    k = jnp.arange(table.shape[0], dtype=I32)
    return jnp.sum(jnp.where(idx[..., None] == k, table, 0), axis=-1).astype(I32)


def _plan_steps(tab, counts, n_tiles):
    tile = jnp.arange(n_tiles, dtype=I32)
    n_used = tab[TAB_USED, 0]
    t_cls = tab[TAB_CLS, :n_tiles]
    t_ord = tab[TAB_ORD, :n_tiles]
    live = tile < n_used
    key = jnp.where(live, t_cls * n_tiles + tile, N_CLASSES * n_tiles + tile)
    pos = jnp.sum(key[None, :] < key[:, None], axis=1).astype(I32)
    step = jnp.arange(n_tiles + 1, dtype=I32)
    step_c = jnp.minimum(step, n_used - 1)
    step_tile = jnp.sum(jnp.where(pos[None, :] == step_c[:, None], tile[None, :], 0), axis=1).astype(I32)
    s_cls = _lookup(t_cls, step_tile)
    s_ord = _lookup(t_ord, step_tile)
    s_nv = jnp.clip(_lookup(counts, s_cls) - s_ord * MOE_TILE, 0, MOE_TILE).astype(I32)
    cls_ea = np.array([g * EXPERTS_PER_GROUP + a for g in range(N_GROUPS) for a, _ in PAIRS], np.int32)
    cls_eb = np.array([g * EXPERTS_PER_GROUP + b for g in range(N_GROUPS) for _, b in PAIRS], np.int32)
    return step_tile, _lookup(cls_ea, s_cls), _lookup(cls_eb, s_cls), s_nv, n_used.reshape(1).astype(I32)


def kernel(x_prompt, x_sample, cache_k, cache_v, state_hgrn, norm1_g, w_in, attn_sink, lower_bounds,
           hgrn_norm_g, w_o, norm2_g, w_router_group, b_router_group, w_router_expert, b_router_expert,
           w_gate, w_up, w_down, final_norm_g):
    depth = w_in.shape[0]
    assert depth == 1
    l = 0
    w_hist = cache_k.shape[2]
    assert w_hist == WINDOW
    w_r = jnp.concatenate(
        [w_router_group[l], jnp.transpose(w_router_expert[l], (1, 0, 2)).reshape(D_MODEL, N_EXPERTS)], axis=1)
    w_r = jnp.pad(w_r, ((0, 0), (0, LANES - w_r.shape[1]))).astype(BF16)
    b_r = jnp.concatenate([b_router_group[l], b_router_expert[l].reshape(N_EXPERTS)])
    b_r = jnp.pad(b_r, (0, LANES - b_r.shape[0])).reshape(1, LANES).astype(F32)
    weights = (norm1_g[l].reshape(1, D_MODEL), w_in[l].astype(BF16), attn_sink[l].astype(F32),
               lower_bounds.astype(F32), hgrn_norm_g[l].reshape(1, B_WIDTH), w_o[l].astype(BF16),
               norm2_g[l].reshape(1, D_MODEL), w_r, b_r)
    g2 = norm2_g[l].reshape(1, D_MODEL)
    gf = final_norm_g.reshape(1, D_MODEL)
    wg, wu, wd = w_gate[l].astype(BF16), w_up[l].astype(BF16), w_down[l].astype(BF16)

    bp, lp, _ = x_prompt.shape
    bs, ls, _ = x_sample.shape
    np_tok, ns_tok = bp * lp, bs * ls
    tbp, tbs = _pick_block(lp, LAYER_BLOCK), _pick_block(ls, LAYER_BLOCK)
    assert tbp % MOE_TILE == 0 and np_tok % MOE_TILE == 0 and np_tok < TOK_SPLIT * TOK_SPLIT

    n_tiles = np_tok // MOE_TILE + N_CLASSES
    assert n_tiles <= TILE_LANES
    xs, tab, cnt, kp, vp, sp = _layer_call(x_prompt, None, weights, 0, tbp, n_tiles + tbp // MOE_TILE)
    counts = cnt[:N_CLASSES, 0].astype(I32)
    step_blk, step_ea, step_eb, step_nv, n_used = _plan_steps(tab, counts, n_tiles)
    yp = _moe_sorted_call(step_blk, step_ea, step_eb, step_nv, n_used, xs, g2, wg, wu, wd, gf, np_tok)

    cache = (cache_k[l].reshape(bs, w_hist, A_KV_WIDTH), cache_v[l].reshape(bs, w_hist, A_KV_WIDTH), state_hgrn[l])
    x1s, routes, kn, vn, sn = _layer_call(x_sample, cache, weights, PAST_LEN, tbs)
    ys = _moe_dense_call(x1s.reshape(ns_tok, D_MODEL), routes.reshape(ns_tok, LANES), g2, wg, wu, wd, gf,
                         _pick_block(ns_tok, LAYER_BLOCK))

    kv_shape = (1, -1, w_hist, A_KV_HEADS, A_HEAD_DIM)
    return (yp.reshape(bp, lp, D_MODEL), ys.reshape(bs, ls, D_MODEL),
            kp.reshape(kv_shape), vp.reshape(kv_shape), sp[None],
            kn.reshape(kv_shape), vn.reshape(kv_shape), sn[None])
```

```python
import functools

import numpy as np
import jax
import jax.numpy as jnp
from jax import lax
from jax.experimental import pallas as pl
from jax.experimental.pallas import tpu as pltpu

F32 = jnp.float32
BF16 = jnp.bfloat16
I32 = jnp.int32

D_MODEL = 1024
CHUNK = 64
EPS = 1e-6
PAST_LEN = 4096
WINDOW = 128
A_HEADS = 8
A_KV_HEADS = 2
A_HEAD_DIM = 64
A_GROUP = A_HEADS // A_KV_HEADS
A_WIDTH = A_HEADS * A_HEAD_DIM
A_KV_WIDTH = A_KV_HEADS * A_HEAD_DIM
B_HEADS = 4
B_KEY_DIM = 128
B_VAL_DIM = 128
B_KEY_WIDTH = B_HEADS * B_KEY_DIM
B_WIDTH = B_HEADS * B_VAL_DIM
MIX_WIDTH = A_WIDTH + B_WIDTH
OFF_K = A_WIDTH
OFF_V = OFF_K + A_KV_WIDTH
OFF_QB = OFF_V + A_KV_WIDTH
OFF_FB = OFF_QB + B_KEY_WIDTH
OFF_IB = OFF_FB + B_KEY_WIDTH
OFF_GB = OFF_IB + B_WIDTH
N_IN = OFF_GB + B_WIDTH
N_GROUPS = 4
EXPERTS_PER_GROUP = 4
N_EXPERTS = N_GROUPS * EXPERTS_PER_GROUP
D_EXPERT = 256
LANES = 128
SUBLANES = 8
ROUTE_E0 = N_GROUPS
PAIRS = [(a, b) for a in range(EXPERTS_PER_GROUP) for b in range(a + 1, EXPERTS_PER_GROUP)]
N_PAIRS = len(PAIRS)
N_CLASSES = N_GROUPS * N_PAIRS
CLS_ROWS = 32
ROW_WIDTH = D_MODEL + LANES
META_GA, META_GB, META_TOK_HI, META_TOK_LO = 0, 1, 2, 3
TOK_SPLIT = 256
LAYER_BLOCK = 512
CHUNK_UNROLL = 4
MOE_TILE = 256
MOE_TILE_SHIFT = 8
TILE_LANES = 256
TAB_CLS, TAB_ORD, TAB_USED = 0, 1, 2

VMEM_LIMIT_BYTES = 56 * 1024 * 1024

NT_DIMS = (((1,), (1,)), ((), ()))
TN_DIMS = (((0,), (0,)), ((), ()))


def _rmsnorm(x, g):
    return x * lax.rsqrt(jnp.mean(x * x, axis=-1, keepdims=True) + EPS) * g


def _silu(x):
    hx = 0.5 * x
    return hx + hx * jnp.tanh(hx)


def _route(logits):
    lane = lax.broadcasted_iota(I32, logits.shape, 1)
    neg = -jnp.inf
    is_g = lane < N_GROUPS
    gl = jnp.where(is_g, logits, neg)
    gmax = jnp.max(gl, axis=-1, keepdims=True)
    gidx = jnp.min(jnp.where(gl == gmax, lane, LANES), axis=-1, keepdims=True)
    p_group = 1.0 / jnp.sum(jnp.where(is_g, jnp.exp(logits - gmax), 0.0), axis=-1, keepdims=True)
    e_lo = ROUTE_E0 + EXPERTS_PER_GROUP * gidx
    in_group = (lane >= e_lo) & (lane < e_lo + EXPERTS_PER_GROUP)
    el = jnp.where(in_group, logits, neg)
    e1 = jnp.max(el, axis=-1, keepdims=True)
    i1 = jnp.min(jnp.where(el == e1, lane, LANES), axis=-1, keepdims=True)
    el2 = jnp.where(lane == i1, neg, el)
    e2 = jnp.max(el2, axis=-1, keepdims=True)
    i2 = jnp.min(jnp.where(el2 == e2, lane, LANES), axis=-1, keepdims=True)
    t = jnp.exp(e2 - e1)
    w1 = p_group / (1.0 + t)
    w2 = w1 * t
    first_low = i1 < i2
    ea = jnp.where(first_low, i1, i2) - e_lo
    eb = jnp.where(first_low, i2, i1) - e_lo
    ga = jnp.where(first_low, w1, w2)
    gb = jnp.where(first_low, w2, w1)
    pair = ((ea * (2 * EXPERTS_PER_GROUP - 1 - ea)) >> 1) + eb - ea - 1
    dense = jnp.where(lane == i1, w1, 0.0) + jnp.where(lane == i2, w2, 0.0)
    return gidx * N_PAIRS + pair, ga, gb, dense


def _mixer_and_router(pos0, tb, j, x_ref, g1_ref, win_ref, sink_ref, lb_ref, hg_ref, wo_ref, g2_ref, wr_ref, br_ref,
                      proj_s, kbuf, vbuf, st_s, mix_s):
    nc = tb // CHUNK
    nk = WINDOW + CHUNK
    x = x_ref[...]
    h = _rmsnorm(x, g1_ref[...]).astype(BF16)
    proj_s[...] = jnp.dot(h, win_ref[...], preferred_element_type=F32)
    kbuf[WINDOW:WINDOW + tb, :] = proj_s[:, OFF_K:OFF_K + A_KV_WIDTH]
    vbuf[WINDOW:WINDOW + tb, :] = proj_s[:, OFF_V:OFF_V + A_KV_WIDTH]

    row = lax.broadcasted_iota(I32, (A_GROUP * CHUNK, nk), 0)
    col = lax.broadcasted_iota(I32, (A_GROUP * CHUNK, nk), 1)
    dist = jnp.abs((row & (CHUNK - 1)) - (col - WINDOW)).astype(F32)
    row_head = row // CHUNK
    rowc_head = lax.broadcasted_iota(I32, (A_GROUP * CHUNK, 1), 0) // CHUNK
    bias = []
    sinkc = []
    for hk in range(A_KV_HEADS):
        slope = jnp.exp2(-(row_head + (hk * A_GROUP + 1)).astype(F32))
        bias.append(slope * dist)
        sc = jnp.zeros((A_GROUP * CHUNK, 1), F32)
        for g in range(A_GROUP):
            sc = jnp.where(rowc_head == g, sink_ref[hk * A_GROUP + g], sc)
        sinkc.append(sc)

    lbr = lb_ref[...]
    lbm = jnp.max(lbr, axis=0, keepdims=True)
    lbe = jnp.exp(lbr - lbm)
    lb = lbe[0:1, :] / jnp.sum(lbe, axis=0, keepdims=True)
    f_mid = 0.5 * (1.0 + lb)
    f_half = 0.5 * (1.0 - lb)
    hg = hg_ref[...]
    tr = lax.broadcasted_iota(I32, (CHUNK, CHUNK), 0)
    tc = lax.broadcasted_iota(I32, (CHUNK, CHUNK), 1)
    tril = tr >= tc
    ltri = jnp.where(tril, 1.0, 0.0).astype(BF16)

    def chunk_body(c, carry):
        r0 = pl.multiple_of(c * CHUNK, CHUNK)
        rows = pl.ds(r0, CHUNK)
        valid = col >= (WINDOW - pos0) - (j * tb + c * CHUNK)
        for hk in range(A_KV_HEADS):
            qc = proj_s[rows, hk * A_GROUP * A_HEAD_DIM:(hk + 1) * A_GROUP * A_HEAD_DIM] * (A_HEAD_DIM ** -0.5)
            q4 = jnp.concatenate([qc[:, g * A_HEAD_DIM:(g + 1) * A_HEAD_DIM] for g in range(A_GROUP)],
                                 axis=0).astype(BF16)
            kw = kbuf[pl.ds(r0, nk), hk * A_HEAD_DIM:(hk + 1) * A_HEAD_DIM].astype(BF16)
            vw = vbuf[pl.ds(r0, nk), hk * A_HEAD_DIM:(hk + 1) * A_HEAD_DIM].astype(BF16)
            s = lax.dot_general(q4, kw, NT_DIMS, preferred_element_type=F32) - bias[hk]
            s = jnp.where(valid, s, -jnp.inf)
            m = jnp.maximum(jnp.max(s, axis=-1, keepdims=True), sinkc[hk])
            p = jnp.exp(s - m)
            den = jnp.sum(p, axis=-1, keepdims=True) + jnp.exp(sinkc[hk] - m)
            o = jnp.dot(p.astype(BF16), vw, preferred_element_type=F32) / den
            att = jnp.concatenate([o[g * CHUNK:(g + 1) * CHUNK] for g in range(A_GROUP)], axis=1)
            mix_s[rows, hk * A_GROUP * A_HEAD_DIM:(hk + 1) * A_GROUP * A_HEAD_DIM] = att.astype(BF16)

        qb = proj_s[rows, OFF_QB:OFF_QB + B_KEY_WIDTH]
        fl = proj_s[rows, OFF_FB:OFF_FB + B_KEY_WIDTH]
        vb = proj_s[rows, OFF_IB:OFF_IB + B_WIDTH].astype(BF16)
        gg = proj_s[rows, OFF_GB:OFF_GB + B_WIDTH]
        f = f_mid + f_half * jnp.tanh(0.5 * fl)
        logf = jnp.log(f)
        hi = logf.astype(BF16)
        lo = (logf - hi.astype(F32)).astype(BF16)
        cum = (jnp.dot(ltri, hi, preferred_element_type=F32) + jnp.dot(ltri, lo, preferred_element_type=F32))
        ref = cum[CHUNK // 2:CHUNK // 2 + 1, :]
        tot = cum[CHUNK - 1:CHUNK, :]
        qf = _silu(qb)
        kf = 1.0 - f
        e_fwd = jnp.exp(cum - ref)
        e_bwd = 1.0 / e_fwd
        q_rel = qf * e_fwd
        k_rel = kf * e_bwd
        q_in = q_rel.astype(BF16)
        k_in = k_rel.astype(BF16)
        k_tot = (k_rel * jnp.exp(tot - ref)).astype(BF16)
        q_cum = (q_rel * jnp.exp(ref)).astype(BF16)
        dec = jnp.exp(tot)
        outs = []
        for hh in range(B_HEADS):
            sl = slice(hh * B_KEY_DIM, (hh + 1) * B_KEY_DIM)
            a = lax.dot_general(q_in[:, sl], k_in[:, sl], NT_DIMS, preferred_element_type=F32)
            a = jnp.where(tril, a, 0.0).astype(BF16)
            st = st_s[hh]
            o = (jnp.dot(a, vb[:, sl], preferred_element_type=F32)
                 + lax.dot_general(q_cum[:, sl], st.astype(BF16), NT_DIMS, preferred_element_type=F32))
            ds_t = lax.dot_general(vb[:, sl], k_tot[:, sl], TN_DIMS, preferred_element_type=F32)
            st_s[hh] = st * dec[:, sl] + ds_t
            o = o * lax.rsqrt(jnp.mean(o * o, axis=-1, keepdims=True) + EPS) * hg[:, sl]
            outs.append(o)
        rec = jnp.concatenate(outs, axis=1) * _silu(gg)
        mix_s[rows, A_WIDTH:A_WIDTH + B_WIDTH] = rec.astype(BF16)
        return carry

    lax.fori_loop(0, nc, chunk_body, 0, unroll=min(nc, CHUNK_UNROLL))

    x1 = x + jnp.dot(mix_s[...], wo_ref[...], preferred_element_type=F32)
    h2 = _rmsnorm(x1, g2_ref[...]).astype(BF16)
    logits = jnp.dot(h2, wr_ref[...], preferred_element_type=F32) + br_ref[...]
    return x1, logits


def _row_copy(src_ref, src_row, dst_ref, dst_row, sem):
    return pltpu.make_async_copy(src_ref.at[pl.ds(src_row, 1)], dst_ref.at[pl.ds(dst_row, 1)], sem)


def _issue_rows(n, src_ref, src_slot, dst_ref, idx_ref, sem, first=0):
    for r in range(first, n):
        _row_copy(src_ref.at[src_slot], r, dst_ref, idx_ref[src_slot, r], sem.at[src_slot]).start(priority=r % 2)


def _layer_body(pos0, tb, sorted_out, *refs):
    if sorted_out:
        (x_ref, g1_ref, win_ref, sink_ref, lb_ref, hg_ref, wo_ref, g2_ref, wr_ref, br_ref,
         xs_ref, tab_ref, cnt_ref, kwin_ref, vwin_ref, sout_ref,
         proj_s, kbuf, vbuf, st_s, mix_s, xrow_s, cnt_s, cur_s, tab_s, dest_v, dest_sm, dsem, rsem) = refs
    else:
        (x_ref, ck_ref, cv_ref, s0_ref, g1_ref, win_ref, sink_ref, lb_ref, hg_ref, wo_ref, g2_ref, wr_ref, br_ref,
         x1_ref, route_ref, kwin_ref, vwin_ref, sout_ref,
         proj_s, kbuf, vbuf, st_s, mix_s) = refs
    b = pl.program_id(0)
    j = pl.program_id(1)
    nblk = pl.num_programs(1)
    step = b * nblk + j

    if sorted_out:
        slot = step % 2
        prev = 1 - slot
        n_tiles = xs_ref.shape[0] // MOE_TILE
        spare_row0 = (n_tiles - tb // MOE_TILE) * MOE_TILE

        @pl.when(step == 0)
        def _init_routing():
            cnt_s[...] = jnp.zeros(cnt_s.shape, F32)
            cur_s[...] = jnp.zeros(cur_s.shape, F32)
            tab_s[...] = jnp.zeros(tab_s.shape, I32)
            xrow_s[1] = jnp.zeros(xrow_s.shape[1:], F32)

            def fill(r, carry):
                dest_sm[1, r] = spare_row0 + r
                return carry

            lax.fori_loop(0, tb, fill, 0)

    @pl.when(j == 0)
    def _init_stream():
        if sorted_out:
            kbuf[0:WINDOW, :] = jnp.zeros((WINDOW, A_KV_WIDTH), F32)
            vbuf[0:WINDOW, :] = jnp.zeros((WINDOW, A_KV_WIDTH), F32)
            st_s[...] = jnp.zeros(st_s.shape, F32)
        else:
            kbuf[0:WINDOW, :] = ck_ref[...]
            vbuf[0:WINDOW, :] = cv_ref[...]
            for hh in range(B_HEADS):
                st_s[hh] = s0_ref[hh].T

    if sorted_out:
        _issue_rows(tb, xrow_s, prev, xs_ref, dest_sm, rsem)

    x1, logits = _mixer_and_router(pos0, tb, j, x_ref, g1_ref, win_ref, sink_ref, lb_ref, hg_ref, wo_ref, g2_ref,
                                   wr_ref, br_ref, proj_s, kbuf, vbuf, st_s, mix_s)
    cls, ga, gb, dense = _route(logits)

    if not sorted_out:
        x1_ref[...] = x1
        route_ref[...] = dense
    else:
        lane = lax.broadcasted_iota(I32, (tb, LANES), 1)
        tok = step * tb + lax.broadcasted_iota(I32, (tb, 1), 0)
        tok_hi = (tok >> 8).astype(F32)
        tok_lo = (tok & (TOK_SPLIT - 1)).astype(F32)
        meta = jnp.where(lane == META_GA, ga,
                         jnp.where(lane == META_GB, gb,
                                   jnp.where(lane == META_TOK_HI, tok_hi,
                                             jnp.where(lane == META_TOK_LO, tok_lo, 0.0))))
        xrow_s[slot, :, 0:D_MODEL] = x1
        xrow_s[slot, :, D_MODEL:ROW_WIDTH] = meta

        onehot = jnp.where(lane == cls, 1.0, 0.0).astype(BF16)
        er = lax.broadcasted_iota(I32, (CLS_ROWS, LANES), 0)
        ec = lax.broadcasted_iota(I32, (CLS_ROWS, LANES), 1)
        eye = jnp.where(er == ec, 1.0, 0.0).astype(BF16)
        oht = lax.dot_general(eye, onehot, NT_DIMS, preferred_element_type=F32)
        ur = lax.broadcasted_iota(I32, (tb, tb), 0)
        uc = lax.broadcasted_iota(I32, (tb, tb), 1)
        before = jnp.where(ur < uc, 1.0, 0.0).astype(BF16)
        prefix = jnp.dot(oht.astype(BF16), before, preferred_element_type=F32)
        cnt = cnt_s[:, 0:1]
        rank = jnp.sum(oht * (prefix + cnt), axis=0, keepdims=True).astype(I32)

        n_c = jnp.sum(oht, axis=1, keepdims=True)
        cnt_i = cnt.astype(I32)
        after_i = (cnt + n_c).astype(I32)
        tiles_before = (cnt_i + (MOE_TILE - 1)) >> MOE_TILE_SHIFT
        new_c = ((after_i + (MOE_TILE - 1)) >> MOE_TILE_SHIFT) - tiles_before
        sr = lax.broadcasted_iota(I32, (CLS_ROWS, CLS_ROWS), 0)
        sc = lax.broadcasted_iota(I32, (CLS_ROWS, CLS_ROWS), 1)
        lower = jnp.where(sr > sc, 1.0, 0.0).astype(BF16)
        new_b = jnp.broadcast_to(new_c.astype(F32), (CLS_ROWS, LANES)).astype(BF16)
        opened_before = jnp.dot(lower, new_b, preferred_element_type=F32)[:, 0:1].astype(I32)
        used = tab_s[TAB_USED:TAB_USED + 1, 0:1]
        base_c = used + opened_before
        partial = (cnt_i & (MOE_TILE - 1)) != 0
        cur_c = cur_s[:, 0:1].astype(I32)
        first_c = jnp.where(partial, cur_c, base_c)
        shift_c = jnp.where(partial, base_c - 1, base_c)
        ord0_c = cnt_i >> MOE_TILE_SHIFT

        def per_token(col):
            return jnp.sum(oht * col.astype(F32), axis=0, keepdims=True).astype(I32)

        ord_t = rank >> MOE_TILE_SHIFT
        d_t = ord_t - per_token(ord0_c)
        tile_t = jnp.where(d_t == 0, per_token(first_c), per_token(shift_c) + d_t)
        dest = tile_t * MOE_TILE + (rank & (MOE_TILE - 1))

        tl = lax.broadcasted_iota(I32, (CLS_ROWS, TILE_LANES), 1)
        opened = (tl >= base_c) & (tl < base_c + new_c)
        cls_id = lax.broadcasted_iota(I32, (CLS_ROWS, TILE_LANES), 0)
        hit = jnp.sum(jnp.where(opened, 1, 0), axis=0, keepdims=True) > 0
        t_cls = jnp.sum(jnp.where(opened, cls_id, 0), axis=0, keepdims=True)
        t_ord = jnp.sum(jnp.where(opened, tiles_before + (tl - base_c), 0), axis=0, keepdims=True)
        tab_s[TAB_CLS:TAB_CLS + 1, :] = jnp.where(hit, t_cls, tab_s[TAB_CLS:TAB_CLS + 1, :])
        tab_s[TAB_ORD:TAB_ORD + 1, :] = jnp.where(hit, t_ord, tab_s[TAB_ORD:TAB_ORD + 1, :])
        used_new = used + jnp.sum(new_c, axis=0, keepdims=True)
        tab_s[TAB_USED:TAB_USED + 1, :] = jnp.broadcast_to(used_new, (1, TILE_LANES))
        cur_s[...] = jnp.broadcast_to(jnp.where(new_c > 0, base_c + new_c - 1, cur_c).astype(F32), cur_s.shape)
        cnt_s[...] = jnp.broadcast_to(cnt + n_c, cnt_s.shape)
        tab_ref[...] = tab_s[...]
        cnt_ref[...] = cnt_s[...]

        dest_v[...] = jnp.broadcast_to(dest, dest_v.shape)
        to_smem = pltpu.make_async_copy(dest_v.at[pl.ds(0, 1)], dest_sm.at[pl.ds(slot, 1)], dsem)
        to_smem.start()
        to_smem.wait()

        def wait_rows(s):
            pltpu.make_async_copy(xrow_s.at[s], xs_ref.at[pl.ds(0, tb)], rsem.at[s]).wait()

        wait_rows(prev)

        @pl.when(step == pl.num_programs(0) * nblk - 1)
        def _flush():
            _issue_rows(tb, xrow_s, slot, xs_ref, dest_sm, rsem)
            wait_rows(slot)

    kt = kbuf[tb:tb + WINDOW, :]
    vt = vbuf[tb:tb + WINDOW, :]
    kbuf[0:WINDOW, :] = kt
    vbuf[0:WINDOW, :] = vt
    kwin_ref[...] = kt
    vwin_ref[...] = vt
    for hh in range(B_HEADS):
        sout_ref[hh] = st_s[hh].T


def _layer_call(x, cache, weights, pos0, tb, n_sorted_tiles=None):
    bsz, seq, _ = x.shape
    nblk = seq // tb
    sorted_out = cache is None
    g1, w_in, sink, lower_bounds, hg, w_o, g2, w_r, b_r = weights

    def const(shape):
        return pl.BlockSpec(shape, lambda b, j: (0,) * len(shape))

    def per_stream(shape):
        return pl.BlockSpec((None,) + shape, lambda b, j: (b,) + (0,) * len(shape))

    in_specs = [pl.BlockSpec((None, tb, D_MODEL), lambda b, j: (b, j, 0))]
    args = [x]
    if not sorted_out:
        in_specs += [per_stream((WINDOW, A_KV_WIDTH)), per_stream((WINDOW, A_KV_WIDTH)),
                     per_stream((B_HEADS, B_KEY_DIM, B_VAL_DIM))]
        args += list(cache)
    in_specs += [const((1, D_MODEL)), const((D_MODEL, N_IN)),
                 pl.BlockSpec(memory_space=pltpu.SMEM),
                 const((2, B_KEY_WIDTH)), const((1, B_WIDTH)), const((MIX_WIDTH, D_MODEL)),
                 const((1, D_MODEL)), const((D_MODEL, LANES)), const((1, LANES))]
    args += [g1, w_in, sink, lower_bounds, hg, w_o, g2, w_r, b_r]
    stream_shapes = (jax.ShapeDtypeStruct((bsz, WINDOW, A_KV_WIDTH), F32),
                     jax.ShapeDtypeStruct((bsz, WINDOW, A_KV_WIDTH), F32),
                     jax.ShapeDtypeStruct((bsz, B_HEADS, B_KEY_DIM, B_VAL_DIM), F32))
    stream_specs = (per_stream((WINDOW, A_KV_WIDTH)), per_stream((WINDOW, A_KV_WIDTH)),
                    per_stream((B_HEADS, B_KEY_DIM, B_VAL_DIM)))
    scratch = [pltpu.VMEM((tb, N_IN), F32),
               pltpu.VMEM((WINDOW + tb, A_KV_WIDTH), F32),
               pltpu.VMEM((WINDOW + tb, A_KV_WIDTH), F32),
               pltpu.VMEM((B_HEADS, B_VAL_DIM, B_KEY_DIM), F32),
               pltpu.VMEM((tb, MIX_WIDTH), BF16)]
    if sorted_out:
        out_shape = (jax.ShapeDtypeStruct((n_sorted_tiles * MOE_TILE, ROW_WIDTH), F32),
                     jax.ShapeDtypeStruct((SUBLANES, TILE_LANES), I32),
                     jax.ShapeDtypeStruct((CLS_ROWS, LANES), F32)) + stream_shapes
        out_specs = (pl.BlockSpec(memory_space=pl.ANY), const((SUBLANES, TILE_LANES)),
                     const((CLS_ROWS, LANES))) + stream_specs
        scratch += [pltpu.VMEM((2, tb, ROW_WIDTH), F32),
                    pltpu.VMEM((CLS_ROWS, LANES), F32),
                    pltpu.VMEM((CLS_ROWS, LANES), F32),
                    pltpu.VMEM((SUBLANES, TILE_LANES), I32),
                    pltpu.VMEM((SUBLANES, tb), I32),
                    pltpu.SMEM((2, tb), I32),
                    pltpu.SemaphoreType.DMA(()),
                    pltpu.SemaphoreType.DMA((2,))]
    else:
        out_shape = (jax.ShapeDtypeStruct((bsz, seq, D_MODEL), F32),
                     jax.ShapeDtypeStruct((bsz, seq, LANES), F32)) + stream_shapes
        out_specs = (pl.BlockSpec((None, tb, D_MODEL), lambda b, j: (b, j, 0)),
                     pl.BlockSpec((None, tb, LANES), lambda b, j: (b, j, 0))) + stream_specs
    return pl.pallas_call(
        functools.partial(_layer_body, pos0, tb, sorted_out),
        grid=(bsz, nblk),
        in_specs=in_specs,
        out_specs=out_specs,
        out_shape=out_shape,
        scratch_shapes=scratch,
        compiler_params=pltpu.CompilerParams(
            dimension_semantics=("arbitrary", "arbitrary"),
            vmem_limit_bytes=VMEM_LIMIT_BYTES),
        name="layer_prompt" if sorted_out else "layer_sample",
    )(*args)


def _moe_sorted_body(blk_ref, ea_ref, eb_ref, nv_ref, nused_ref, xs_ref, g2_ref,
                     wga_ref, wua_ref, wda_ref, wgb_ref, wub_ref, wdb_ref, gf_ref, y_ref,
                     ybuf, h2_s, ya_s, tok_v, tok_sm, tsem, rsem):
    p = pl.program_id(0)
    n_used = nused_ref[0]
    slot = p % 2
    prev = 1 - slot
    nv_prev = nv_ref[jnp.maximum(p - 1, 0)]

    def expert(h2, wg_ref, wu_ref, wd_ref):
        a = jnp.dot(h2, wg_ref[...], preferred_element_type=F32)
        u = jnp.dot(h2, wu_ref[...], preferred_element_type=F32)
        return jnp.dot((_silu(a) * u).astype(BF16), wd_ref[...], preferred_element_type=F32)

    def first_half():
        h2 = _rmsnorm(xs_ref[:, 0:D_MODEL], g2_ref[...]).astype(BF16)
        h2_s[...] = h2
        ya_s[...] = expert(h2, wga_ref, wua_ref, wda_ref)

    def second_half():
        x1 = xs_ref[:, 0:D_MODEL]
        meta = xs_ref[:, D_MODEL:ROW_WIDTH]
        yb = expert(h2_s[...], wgb_ref, wub_ref, wdb_ref)
        moe = meta[:, META_GA:META_GA + 1] * ya_s[...] + meta[:, META_GB:META_GB + 1] * yb
        ybuf[slot] = _rmsnorm(x1 + moe, gf_ref[...])
        sr = lax.broadcasted_iota(I32, (SUBLANES, LANES), 0)
        sc = lax.broadcasted_iota(I32, (SUBLANES, LANES), 1)
        pick = jnp.where(sc == sr + META_TOK_HI, 1.0, 0.0).astype(BF16)
        ids = lax.dot_general(pick, meta.astype(BF16), NT_DIMS, preferred_element_type=F32)
        tok = (ids[0:1, :] * TOK_SPLIT + ids[1:2, :]).astype(I32)
        tok_v[...] = jnp.broadcast_to(tok, tok_v.shape)
        to_smem = pltpu.make_async_copy(tok_v.at[pl.ds(0, 1)], tok_sm.at[pl.ds(slot, 1)], tsem)
        to_smem.start()
        to_smem.wait()

    def send_full():
        _issue_rows(MOE_TILE // 2, ybuf, prev, y_ref, tok_sm, rsem)

    def send_full_rest():
        _issue_rows(MOE_TILE, ybuf, prev, y_ref, tok_sm, rsem, first=MOE_TILE // 2)

    def wait_full():
        pltpu.make_async_copy(ybuf.at[prev], y_ref.at[pl.ds(0, MOE_TILE)], rsem.at[prev]).wait()

    def send_part():
        def start(r, carry):
            _row_copy(ybuf.at[prev], r, y_ref, tok_sm[prev, r], rsem.at[prev]).start()
            return carry

        lax.fori_loop(0, nv_prev, start, 0)

    def wait_part():
        def wait(r, carry):
            _row_copy(ybuf.at[prev], r, y_ref, tok_sm[prev, r], rsem.at[prev]).wait()
            return carry

        lax.fori_loop(0, nv_prev, wait, 0)

    has_prev = (p >= 1) & (p <= n_used)
    prev_full = nv_prev == MOE_TILE
    prev_part = jnp.logical_not(prev_full)
    active = p < n_used
    idle = jnp.logical_not(active)
    steady = active & has_prev & prev_full

    @pl.when(steady)
    def _steady_first():
        send_full()
        first_half()

    @pl.when(active & has_prev & prev_part)
    def _after_partial():
        send_part()
        first_half()

    @pl.when(active & jnp.logical_not(has_prev))
    def _first():
        first_half()

    @pl.when(idle & has_prev & prev_full)
    def _flush_full():
        send_full()
        send_full_rest()

    @pl.when(idle & has_prev & prev_part)
    def _flush_part():
        send_part()

    @pl.when(steady)
    def _steady_second():
        send_full_rest()
        second_half()

    @pl.when(active & jnp.logical_not(steady))
    def _second():
        second_half()

    @pl.when(has_prev & prev_full)
    def _wait_full():
        wait_full()

    @pl.when(has_prev & prev_part)
    def _wait_part():
        wait_part()


def _moe_sorted_call(step_blk, step_ea, step_eb, step_nv, n_used, xs, g2, wg, wu, wd, gf, n_tok):
    n_steps = step_blk.shape[0]

    def const(shape):
        return pl.BlockSpec(shape, lambda p, blk, ea, eb, nv, nu: (0,) * len(shape))

    def w_a(shape):
        return pl.BlockSpec((None,) + shape, lambda p, blk, ea, eb, nv, nu: (ea[p], 0, 0))

    def w_b(shape):
        return pl.BlockSpec((None,) + shape, lambda p, blk, ea, eb, nv, nu: (eb[p], 0, 0))

    return pl.pallas_call(
        _moe_sorted_body,
        grid_spec=pltpu.PrefetchScalarGridSpec(
            num_scalar_prefetch=5,
            grid=(n_steps,),
            in_specs=[pl.BlockSpec((MOE_TILE, ROW_WIDTH), lambda p, blk, ea, eb, nv, nu: (blk[p], 0)),
                      const((1, D_MODEL)),
                      w_a((D_MODEL, D_EXPERT)), w_a((D_MODEL, D_EXPERT)), w_a((D_EXPERT, D_MODEL)),
                      w_b((D_MODEL, D_EXPERT)), w_b((D_MODEL, D_EXPERT)), w_b((D_EXPERT, D_MODEL)),
                      const((1, D_MODEL))],
            out_specs=pl.BlockSpec(memory_space=pl.ANY),
            scratch_shapes=[pltpu.VMEM((2, MOE_TILE, D_MODEL), F32),
                            pltpu.VMEM((MOE_TILE, D_MODEL), BF16),
                            pltpu.VMEM((MOE_TILE, D_MODEL), F32),
                            pltpu.VMEM((SUBLANES, MOE_TILE), I32),
                            pltpu.SMEM((2, MOE_TILE), I32),
                            pltpu.SemaphoreType.DMA(()),
                            pltpu.SemaphoreType.DMA((2,))]),
        out_shape=jax.ShapeDtypeStruct((n_tok, D_MODEL), F32),
        compiler_params=pltpu.CompilerParams(
            dimension_semantics=("arbitrary",), vmem_limit_bytes=VMEM_LIMIT_BYTES),
        name="moe_sorted",
    )(step_blk, step_ea, step_eb, step_nv, n_used, xs, g2, wg, wu, wd, wg, wu, wd, gf)


def _moe_dense_body(x1_ref, route_ref, g2_ref, wg_ref, wu_ref, wd_ref, gf_ref, y_ref, h2_s, acc_s):
    e = pl.program_id(1)

    @pl.when(e == 0)
    def _first():
        h2_s[...] = _rmsnorm(x1_ref[...], g2_ref[...]).astype(BF16)
        acc_s[...] = jnp.zeros(acc_s.shape, F32)

    h2 = h2_s[...]
    a = jnp.dot(h2, wg_ref[...], preferred_element_type=F32)
    u = jnp.dot(h2, wu_ref[...], preferred_element_type=F32)
    y = jnp.dot((_silu(a) * u).astype(BF16), wd_ref[...], preferred_element_type=F32)
    route = route_ref[...]
    lane = lax.broadcasted_iota(I32, route.shape, 1)
    gate = jnp.sum(jnp.where(lane == e + ROUTE_E0, route, 0.0), axis=-1, keepdims=True)
    acc_s[...] += gate * y

    @pl.when(e == N_EXPERTS - 1)
    def _last():
        y_ref[...] = _rmsnorm(x1_ref[...] + acc_s[...], gf_ref[...])


def _moe_dense_call(x1, route, g2, wg, wu, wd, gf, tm):
    n = x1.shape[0]
    return pl.pallas_call(
        _moe_dense_body,
        grid=(n // tm, N_EXPERTS),
        in_specs=[pl.BlockSpec((tm, D_MODEL), lambda i, e: (i, 0)),
                  pl.BlockSpec((tm, LANES), lambda i, e: (i, 0)),
                  pl.BlockSpec((1, D_MODEL), lambda i, e: (0, 0)),
                  pl.BlockSpec((None, D_MODEL, D_EXPERT), lambda i, e: (e, 0, 0)),
                  pl.BlockSpec((None, D_MODEL, D_EXPERT), lambda i, e: (e, 0, 0)),
                  pl.BlockSpec((None, D_EXPERT, D_MODEL), lambda i, e: (e, 0, 0)),
                  pl.BlockSpec((1, D_MODEL), lambda i, e: (0, 0))],
        out_specs=pl.BlockSpec((tm, D_MODEL), lambda i, e: (i, 0)),
        out_shape=jax.ShapeDtypeStruct((n, D_MODEL), F32),
        scratch_shapes=[pltpu.VMEM((tm, D_MODEL), BF16), pltpu.VMEM((tm, D_MODEL), F32)],
        compiler_params=pltpu.CompilerParams(
            dimension_semantics=("arbitrary", "arbitrary"),
            vmem_limit_bytes=VMEM_LIMIT_BYTES),
        name="moe_dense",
    )(x1, route, g2, wg, wu, wd, gf)


def _pick_block(seq, target):
    tb = min(seq, target)
    assert seq % tb == 0 and tb % CHUNK == 0
    return tb


def _lookup(table, idx):
    table = jnp.asarray(table, I32)
    k = jnp.arange(table.shape[0], dtype=I32)
    return jnp.sum(jnp.where(idx[..., None] == k, table, 0), axis=-1).astype(I32)


def _plan_steps(tab, counts, n_tiles):
    tile = jnp.arange(n_tiles, dtype=I32)
    n_used = tab[TAB_USED, 0]
    t_cls = tab[TAB_CLS, :n_tiles]
    t_ord = tab[TAB_ORD, :n_tiles]
    live = tile < n_used
    key = jnp.where(live, t_cls * n_tiles + tile, N_CLASSES * n_tiles + tile)
    pos = jnp.sum(key[None, :] < key[:, None], axis=1).astype(I32)
    step = jnp.arange(n_tiles + 1, dtype=I32)
    step_c = jnp.minimum(step, n_used - 1)
    step_tile = jnp.sum(jnp.where(pos[None, :] == step_c[:, None], tile[None, :], 0), axis=1).astype(I32)
    s_cls = _lookup(t_cls, step_tile)
    s_ord = _lookup(t_ord, step_tile)
    s_nv = jnp.clip(_lookup(counts, s_cls) - s_ord * MOE_TILE, 0, MOE_TILE).astype(I32)
    cls_ea = np.array([g * EXPERTS_PER_GROUP + a for g in range(N_GROUPS) for a, _ in PAIRS], np.int32)
    cls_eb = np.array([g * EXPERTS_PER_GROUP + b for g in range(N_GROUPS) for _, b in PAIRS], np.int32)
    return step_tile, _lookup(cls_ea, s_cls), _lookup(cls_eb, s_cls), s_nv, n_used.reshape(1).astype(I32)


def kernel(x_prompt, x_sample, cache_k, cache_v, state_hgrn, norm1_g, w_in, attn_sink, lower_bounds,
           hgrn_norm_g, w_o, norm2_g, w_router_group, b_router_group, w_router_expert, b_router_expert,
           w_gate, w_up, w_down, final_norm_g):
    depth = w_in.shape[0]
    assert depth == 1
    l = 0
    w_hist = cache_k.shape[2]
    assert w_hist == WINDOW
    w_r = jnp.concatenate(
        [w_router_group[l], jnp.transpose(w_router_expert[l], (1, 0, 2)).reshape(D_MODEL, N_EXPERTS)], axis=1)
    w_r = jnp.pad(w_r, ((0, 0), (0, LANES - w_r.shape[1]))).astype(BF16)
    b_r = jnp.concatenate([b_router_group[l], b_router_expert[l].reshape(N_EXPERTS)])
    b_r = jnp.pad(b_r, (0, LANES - b_r.shape[0])).reshape(1, LANES).astype(F32)
    weights = (norm1_g[l].reshape(1, D_MODEL), w_in[l].astype(BF16), attn_sink[l].astype(F32),
               lower_bounds.astype(F32), hgrn_norm_g[l].reshape(1, B_WIDTH), w_o[l].astype(BF16),
               norm2_g[l].reshape(1, D_MODEL), w_r, b_r)
    g2 = norm2_g[l].reshape(1, D_MODEL)
    gf = final_norm_g.reshape(1, D_MODEL)
    wg, wu, wd = w_gate[l].astype(BF16), w_up[l].astype(BF16), w_down[l].astype(BF16)

    bp, lp, _ = x_prompt.shape
    bs, ls, _ = x_sample.shape
    np_tok, ns_tok = bp * lp, bs * ls
    tbp, tbs = _pick_block(lp, LAYER_BLOCK), _pick_block(ls, LAYER_BLOCK)
    assert tbp % MOE_TILE == 0 and np_tok % MOE_TILE == 0 and np_tok < TOK_SPLIT * TOK_SPLIT

    n_tiles = np_tok // MOE_TILE + N_CLASSES
    assert n_tiles <= TILE_LANES
    xs, tab, cnt, kp, vp, sp = _layer_call(x_prompt, None, weights, 0, tbp, n_tiles + tbp // MOE_TILE)
    counts = cnt[:N_CLASSES, 0].astype(I32)
    step_blk, step_ea, step_eb, step_nv, n_used = _plan_steps(tab, counts, n_tiles)
    yp = _moe_sorted_call(step_blk, step_ea, step_eb, step_nv, n_used, xs, g2, wg, wu, wd, gf, np_tok)

    cache = (cache_k[l].reshape(bs, w_hist, A_KV_WIDTH), cache_v[l].reshape(bs, w_hist, A_KV_WIDTH), state_hgrn[l])
    x1s, routes, kn, vn, sn = _layer_call(x_sample, cache, weights, PAST_LEN, tbs)
    ys = _moe_dense_call(x1s.reshape(ns_tok, D_MODEL), routes.reshape(ns_tok, LANES), g2, wg, wu, wd, gf,
                         _pick_block(ns_tok, LAYER_BLOCK))

    kv_shape = (1, -1, w_hist, A_KV_HEADS, A_HEAD_DIM)
    return (yp.reshape(bp, lp, D_MODEL), ys.reshape(bs, ls, D_MODEL),
            kp.reshape(kv_shape), vp.reshape(kv_shape), sp[None],
            kn.reshape(kv_shape), vn.reshape(kv_shape), sn[None])
```

```python
import functools

import numpy as np
import jax
import jax.numpy as jnp
from jax import lax
from jax.experimental import pallas as pl
from jax.experimental.pallas import tpu as pltpu

F32 = jnp.float32
BF16 = jnp.bfloat16
I32 = jnp.int32

D_MODEL = 1024
CHUNK = 64
EPS = 1e-6
PAST_LEN = 4096
WINDOW = 128
A_HEADS = 8
A_KV_HEADS = 2
A_HEAD_DIM = 64
A_GROUP = A_HEADS // A_KV_HEADS
A_WIDTH = A_HEADS * A_HEAD_DIM
A_KV_WIDTH = A_KV_HEADS * A_HEAD_DIM
B_HEADS = 4
B_KEY_DIM = 128
B_VAL_DIM = 128
B_KEY_WIDTH = B_HEADS * B_KEY_DIM
B_WIDTH = B_HEADS * B_VAL_DIM
MIX_WIDTH = A_WIDTH + B_WIDTH
OFF_K = A_WIDTH
OFF_V = OFF_K + A_KV_WIDTH
OFF_QB = OFF_V + A_KV_WIDTH
OFF_FB = OFF_QB + B_KEY_WIDTH
OFF_IB = OFF_FB + B_KEY_WIDTH
OFF_GB = OFF_IB + B_WIDTH
N_IN = OFF_GB + B_WIDTH
N_GROUPS = 4
EXPERTS_PER_GROUP = 4
N_EXPERTS = N_GROUPS * EXPERTS_PER_GROUP
D_EXPERT = 256
LANES = 128
SUBLANES = 8
ROUTE_E0 = N_GROUPS
PAIRS = [(a, b) for a in range(EXPERTS_PER_GROUP) for b in range(a + 1, EXPERTS_PER_GROUP)]
N_PAIRS = len(PAIRS)
N_CLASSES = N_GROUPS * N_PAIRS
CLS_ROWS = 32
ROW_WIDTH = D_MODEL + LANES
META_GA, META_GB, META_TOK_HI, META_TOK_LO = 0, 1, 2, 3
TOK_SPLIT = 256
LAYER_BLOCK = 512
CHUNK_UNROLL = 4
MOE_TILE = 256
MOE_TILE_SHIFT = 8
TILE_LANES = 256
TAB_CLS, TAB_ORD, TAB_USED = 0, 1, 2

VMEM_LIMIT_BYTES = 56 * 1024 * 1024

NT_DIMS = (((1,), (1,)), ((), ()))
TN_DIMS = (((0,), (0,)), ((), ()))


def _rmsnorm(x, g):
    return x * lax.rsqrt(jnp.mean(x * x, axis=-1, keepdims=True) + EPS) * g


def _silu(x):
    hx = 0.5 * x
    return hx + hx * jnp.tanh(hx)


def _route(logits):
    lane = lax.broadcasted_iota(I32, logits.shape, 1)
    neg = -jnp.inf
    is_g = lane < N_GROUPS
    gl = jnp.where(is_g, logits, neg)
    gmax = jnp.max(gl, axis=-1, keepdims=True)
    gidx = jnp.min(jnp.where(gl == gmax, lane, LANES), axis=-1, keepdims=True)
    p_group = 1.0 / jnp.sum(jnp.where(is_g, jnp.exp(logits - gmax), 0.0), axis=-1, keepdims=True)
    e_lo = ROUTE_E0 + EXPERTS_PER_GROUP * gidx
    in_group = (lane >= e_lo) & (lane < e_lo + EXPERTS_PER_GROUP)
    el = jnp.where(in_group, logits, neg)
    e1 = jnp.max(el, axis=-1, keepdims=True)
    i1 = jnp.min(jnp.where(el == e1, lane, LANES), axis=-1, keepdims=True)
    el2 = jnp.where(lane == i1, neg, el)
    e2 = jnp.max(el2, axis=-1, keepdims=True)
    i2 = jnp.min(jnp.where(el2 == e2, lane, LANES), axis=-1, keepdims=True)
    t = jnp.exp(e2 - e1)
    w1 = p_group / (1.0 + t)
    w2 = w1 * t
    first_low = i1 < i2
    ea = jnp.where(first_low, i1, i2) - e_lo
    eb = jnp.where(first_low, i2, i1) - e_lo
    ga = jnp.where(first_low, w1, w2)
    gb = jnp.where(first_low, w2, w1)
    pair = ((ea * (2 * EXPERTS_PER_GROUP - 1 - ea)) >> 1) + eb - ea - 1
    dense = jnp.where(lane == i1, w1, 0.0) + jnp.where(lane == i2, w2, 0.0)
    return gidx * N_PAIRS + pair, ga, gb, dense


def _mixer_and_router(pos0, tb, j, x_ref, g1_ref, win_ref, sink_ref, lb_ref, hg_ref, wo_ref, g2_ref, wr_ref, br_ref,
                      proj_s, kbuf, vbuf, st_s, mix_s):
    nc = tb // CHUNK
    nk = WINDOW + CHUNK
    x = x_ref[...]
    h = _rmsnorm(x, g1_ref[...]).astype(BF16)
    proj_s[...] = jnp.dot(h, win_ref[...], preferred_element_type=F32)
    kbuf[WINDOW:WINDOW + tb, :] = proj_s[:, OFF_K:OFF_K + A_KV_WIDTH]
    vbuf[WINDOW:WINDOW + tb, :] = proj_s[:, OFF_V:OFF_V + A_KV_WIDTH]

    row = lax.broadcasted_iota(I32, (A_GROUP * CHUNK, nk), 0)
    col = lax.broadcasted_iota(I32, (A_GROUP * CHUNK, nk), 1)
    dist = jnp.abs((row & (CHUNK - 1)) - (col - WINDOW)).astype(F32)
    row_head = row // CHUNK
    rowc_head = lax.broadcasted_iota(I32, (A_GROUP * CHUNK, 1), 0) // CHUNK
    bias = []
    sinkc = []
    for hk in range(A_KV_HEADS):
        slope = jnp.exp2(-(row_head + (hk * A_GROUP + 1)).astype(F32))
        bias.append(slope * dist)
        sc = jnp.zeros((A_GROUP * CHUNK, 1), F32)
        for g in range(A_GROUP):
            sc = jnp.where(rowc_head == g, sink_ref[hk * A_GROUP + g], sc)
        sinkc.append(sc)

    lbr = lb_ref[...]
    lbm = jnp.max(lbr, axis=0, keepdims=True)
    lbe = jnp.exp(lbr - lbm)
    lb = lbe[0:1, :] / jnp.sum(lbe, axis=0, keepdims=True)
    f_mid = 0.5 * (1.0 + lb)
    f_half = 0.5 * (1.0 - lb)
    hg = hg_ref[...]
    tr = lax.broadcasted_iota(I32, (CHUNK, CHUNK), 0)
    tc = lax.broadcasted_iota(I32, (CHUNK, CHUNK), 1)
    tril = tr >= tc
    ltri = jnp.where(tril, 1.0, 0.0).astype(BF16)

    def chunk_body(c, carry):
        r0 = pl.multiple_of(c * CHUNK, CHUNK)
        rows = pl.ds(r0, CHUNK)
        valid = col >= (WINDOW - pos0) - (j * tb + c * CHUNK)
        for hk in range(A_KV_HEADS):
            qc = proj_s[rows, hk * A_GROUP * A_HEAD_DIM:(hk + 1) * A_GROUP * A_HEAD_DIM] * (A_HEAD_DIM ** -0.5)
            q4 = jnp.concatenate([qc[:, g * A_HEAD_DIM:(g + 1) * A_HEAD_DIM] for g in range(A_GROUP)],
                                 axis=0).astype(BF16)
            kw = kbuf[pl.ds(r0, nk), hk * A_HEAD_DIM:(hk + 1) * A_HEAD_DIM].astype(BF16)
            vw = vbuf[pl.ds(r0, nk), hk * A_HEAD_DIM:(hk + 1) * A_HEAD_DIM].astype(BF16)
            s = lax.dot_general(q4, kw, NT_DIMS, preferred_element_type=F32) - bias[hk]
            s = jnp.where(valid, s, -jnp.inf)
            m = jnp.maximum(jnp.max(s, axis=-1, keepdims=True), sinkc[hk])
            p = jnp.exp(s - m)
            den = jnp.sum(p, axis=-1, keepdims=True) + jnp.exp(sinkc[hk] - m)
            o = jnp.dot(p.astype(BF16), vw, preferred_element_type=F32) / den
            att = jnp.concatenate([o[g * CHUNK:(g + 1) * CHUNK] for g in range(A_GROUP)], axis=1)
            mix_s[rows, hk * A_GROUP * A_HEAD_DIM:(hk + 1) * A_GROUP * A_HEAD_DIM] = att.astype(BF16)

        qb = proj_s[rows, OFF_QB:OFF_QB + B_KEY_WIDTH]
        fl = proj_s[rows, OFF_FB:OFF_FB + B_KEY_WIDTH]
        vb = proj_s[rows, OFF_IB:OFF_IB + B_WIDTH].astype(BF16)
        gg = proj_s[rows, OFF_GB:OFF_GB + B_WIDTH]
        f = f_mid + f_half * jnp.tanh(0.5 * fl)
        logf = jnp.log(f)
        hi = logf.astype(BF16)
        lo = (logf - hi.astype(F32)).astype(BF16)
        cum = (jnp.dot(ltri, hi, preferred_element_type=F32) + jnp.dot(ltri, lo, preferred_element_type=F32))
        ref = cum[CHUNK // 2:CHUNK // 2 + 1, :]
        tot = cum[CHUNK - 1:CHUNK, :]
        qf = _silu(qb)
        kf = 1.0 - f
        e_fwd = jnp.exp(cum - ref)
        e_bwd = 1.0 / e_fwd
        q_rel = qf * e_fwd
        k_rel = kf * e_bwd
        q_in = q_rel.astype(BF16)
        k_in = k_rel.astype(BF16)
        k_tot = (k_rel * jnp.exp(tot - ref)).astype(BF16)
        q_cum = (q_rel * jnp.exp(ref)).astype(BF16)
        dec = jnp.exp(tot)
        outs = []
        for hh in range(B_HEADS):
            sl = slice(hh * B_KEY_DIM, (hh + 1) * B_KEY_DIM)
            a = lax.dot_general(q_in[:, sl], k_in[:, sl], NT_DIMS, preferred_element_type=F32)
            a = jnp.where(tril, a, 0.0).astype(BF16)
            st = st_s[hh]
            o = (jnp.dot(a, vb[:, sl], preferred_element_type=F32)
                 + lax.dot_general(q_cum[:, sl], st.astype(BF16), NT_DIMS, preferred_element_type=F32))
            ds_t = lax.dot_general(vb[:, sl], k_tot[:, sl], TN_DIMS, preferred_element_type=F32)
            st_s[hh] = st * dec[:, sl] + ds_t
            o = o * lax.rsqrt(jnp.mean(o * o, axis=-1, keepdims=True) + EPS) * hg[:, sl]
            outs.append(o)
        rec = jnp.concatenate(outs, axis=1) * _silu(gg)
        mix_s[rows, A_WIDTH:A_WIDTH + B_WIDTH] = rec.astype(BF16)
        return carry

    lax.fori_loop(0, nc, chunk_body, 0, unroll=min(nc, CHUNK_UNROLL))

    x1 = x + jnp.dot(mix_s[...], wo_ref[...], preferred_element_type=F32)
    h2 = _rmsnorm(x1, g2_ref[...]).astype(BF16)
    logits = jnp.dot(h2, wr_ref[...], preferred_element_type=F32) + br_ref[...]
    return x1, logits


def _row_copy(src_ref, src_row, dst_ref, dst_row, sem):
    return pltpu.make_async_copy(src_ref.at[pl.ds(src_row, 1)], dst_ref.at[pl.ds(dst_row, 1)], sem)


def _issue_rows(n, src_ref, src_slot, dst_ref, idx_ref, sem, first=0):
    for r in range(first, n):
        _row_copy(src_ref.at[src_slot], r, dst_ref, idx_ref[src_slot, r], sem.at[src_slot]).start(priority=r % 2)


def _layer_body(pos0, tb, sorted_out, *refs):
    if sorted_out:
        (x_ref, g1_ref, win_ref, sink_ref, lb_ref, hg_ref, wo_ref, g2_ref, wr_ref, br_ref,
         xs_ref, tab_ref, cnt_ref, kwin_ref, vwin_ref, sout_ref,
         proj_s, kbuf, vbuf, st_s, mix_s, xrow_s, cnt_s, cur_s, tab_s, dest_v, dest_sm, dsem, rsem) = refs
    else:
        (x_ref, ck_ref, cv_ref, s0_ref, g1_ref, win_ref, sink_ref, lb_ref, hg_ref, wo_ref, g2_ref, wr_ref, br_ref,
         x1_ref, route_ref, kwin_ref, vwin_ref, sout_ref,
         proj_s, kbuf, vbuf, st_s, mix_s) = refs
    b = pl.program_id(0)
    j = pl.program_id(1)
    nblk = pl.num_programs(1)
    step = b * nblk + j

    if sorted_out:
        slot = step % 2
        prev = 1 - slot
        n_tiles = xs_ref.shape[0] // MOE_TILE
        spare_row0 = (n_tiles - tb // MOE_TILE) * MOE_TILE

        @pl.when(step == 0)
        def _init_routing():
            cnt_s[...] = jnp.zeros(cnt_s.shape, F32)
            cur_s[...] = jnp.zeros(cur_s.shape, F32)
            tab_s[...] = jnp.zeros(tab_s.shape, I32)
            xrow_s[1] = jnp.zeros(xrow_s.shape[1:], F32)

            def fill(r, carry):
                dest_sm[1, r] = spare_row0 + r
                return carry

            lax.fori_loop(0, tb, fill, 0)

    @pl.when(j == 0)
    def _init_stream():
        if sorted_out:
            kbuf[0:WINDOW, :] = jnp.zeros((WINDOW, A_KV_WIDTH), F32)
            vbuf[0:WINDOW, :] = jnp.zeros((WINDOW, A_KV_WIDTH), F32)
            st_s[...] = jnp.zeros(st_s.shape, F32)
        else:
            kbuf[0:WINDOW, :] = ck_ref[...]
            vbuf[0:WINDOW, :] = cv_ref[...]
            for hh in range(B_HEADS):
                st_s[hh] = s0_ref[hh].T

    if sorted_out:
        _issue_rows(tb, xrow_s, prev, xs_ref, dest_sm, rsem)

    x1, logits = _mixer_and_router(pos0, tb, j, x_ref, g1_ref, win_ref, sink_ref, lb_ref, hg_ref, wo_ref, g2_ref,
                                   wr_ref, br_ref, proj_s, kbuf, vbuf, st_s, mix_s)
    cls, ga, gb, dense = _route(logits)

    if not sorted_out:
        x1_ref[...] = x1
        route_ref[...] = dense
    else:
        lane = lax.broadcasted_iota(I32, (tb, LANES), 1)
        tok = step * tb + lax.broadcasted_iota(I32, (tb, 1), 0)
        tok_hi = (tok >> 8).astype(F32)
        tok_lo = (tok & (TOK_SPLIT - 1)).astype(F32)
        meta = jnp.where(lane == META_GA, ga,
                         jnp.where(lane == META_GB, gb,
                                   jnp.where(lane == META_TOK_HI, tok_hi,
                                             jnp.where(lane == META_TOK_LO, tok_lo, 0.0))))
        xrow_s[slot, :, 0:D_MODEL] = x1
        xrow_s[slot, :, D_MODEL:ROW_WIDTH] = meta

        onehot = jnp.where(lane == cls, 1.0, 0.0).astype(BF16)
        er = lax.broadcasted_iota(I32, (CLS_ROWS, LANES), 0)
        ec = lax.broadcasted_iota(I32, (CLS_ROWS, LANES), 1)
        eye = jnp.where(er == ec, 1.0, 0.0).astype(BF16)
        oht = lax.dot_general(eye, onehot, NT_DIMS, preferred_element_type=F32)
        ur = lax.broadcasted_iota(I32, (tb, tb), 0)
        uc = lax.broadcasted_iota(I32, (tb, tb), 1)
        before = jnp.where(ur < uc, 1.0, 0.0).astype(BF16)
        prefix = jnp.dot(oht.astype(BF16), before, preferred_element_type=F32)
        cnt = cnt_s[:, 0:1]
        rank = jnp.sum(oht * (prefix + cnt), axis=0, keepdims=True).astype(I32)

        n_c = jnp.sum(oht, axis=1, keepdims=True)
        cnt_i = cnt.astype(I32)
        after_i = (cnt + n_c).astype(I32)
        tiles_before = (cnt_i + (MOE_TILE - 1)) >> MOE_TILE_SHIFT
        new_c = ((after_i + (MOE_TILE - 1)) >> MOE_TILE_SHIFT) - tiles_before
        sr = lax.broadcasted_iota(I32, (CLS_ROWS, CLS_ROWS), 0)
        sc = lax.broadcasted_iota(I32, (CLS_ROWS, CLS_ROWS), 1)
        lower = jnp.where(sr > sc, 1.0, 0.0).astype(BF16)
        new_b = jnp.broadcast_to(new_c.astype(F32), (CLS_ROWS, LANES)).astype(BF16)
        opened_before = jnp.dot(lower, new_b, preferred_element_type=F32)[:, 0:1].astype(I32)
        used = tab_s[TAB_USED:TAB_USED + 1, 0:1]
        base_c = used + opened_before
        partial = (cnt_i & (MOE_TILE - 1)) != 0
        cur_c = cur_s[:, 0:1].astype(I32)
        first_c = jnp.where(partial, cur_c, base_c)
        shift_c = jnp.where(partial, base_c - 1, base_c)
        ord0_c = cnt_i >> MOE_TILE_SHIFT

        def per_token(col):
            return jnp.sum(oht * col.astype(F32), axis=0, keepdims=True).astype(I32)

        ord_t = rank >> MOE_TILE_SHIFT
        d_t = ord_t - per_token(ord0_c)
        tile_t = jnp.where(d_t == 0, per_token(first_c), per_token(shift_c) + d_t)
        dest = tile_t * MOE_TILE + (rank & (MOE_TILE - 1))

        tl = lax.broadcasted_iota(I32, (CLS_ROWS, TILE_LANES), 1)
        opened = (tl >= base_c) & (tl < base_c + new_c)
        cls_id = lax.broadcasted_iota(I32, (CLS_ROWS, TILE_LANES), 0)
        hit = jnp.sum(jnp.where(opened, 1, 0), axis=0, keepdims=True) > 0
        t_cls = jnp.sum(jnp.where(opened, cls_id, 0), axis=0, keepdims=True)
        t_ord = jnp.sum(jnp.where(opened, tiles_before + (tl - base_c), 0), axis=0, keepdims=True)
        tab_s[TAB_CLS:TAB_CLS + 1, :] = jnp.where(hit, t_cls, tab_s[TAB_CLS:TAB_CLS + 1, :])
        tab_s[TAB_ORD:TAB_ORD + 1, :] = jnp.where(hit, t_ord, tab_s[TAB_ORD:TAB_ORD + 1, :])
        used_new = used + jnp.sum(new_c, axis=0, keepdims=True)
        tab_s[TAB_USED:TAB_USED + 1, :] = jnp.broadcast_to(used_new, (1, TILE_LANES))
        cur_s[...] = jnp.broadcast_to(jnp.where(new_c > 0, base_c + new_c - 1, cur_c).astype(F32), cur_s.shape)
        cnt_s[...] = jnp.broadcast_to(cnt + n_c, cnt_s.shape)
        tab_ref[...] = tab_s[...]
        cnt_ref[...] = cnt_s[...]

        dest_v[...] = jnp.broadcast_to(dest, dest_v.shape)
        to_smem = pltpu.make_async_copy(dest_v.at[pl.ds(0, 1)], dest_sm.at[pl.ds(slot, 1)], dsem)
        to_smem.start()
        to_smem.wait()

        def wait_rows(s):
            pltpu.make_async_copy(xrow_s.at[s], xs_ref.at[pl.ds(0, tb)], rsem.at[s]).wait()

        wait_rows(prev)

        @pl.when(step == pl.num_programs(0) * nblk - 1)
        def _flush():
            _issue_rows(tb, xrow_s, slot, xs_ref, dest_sm, rsem)
            wait_rows(slot)

    kt = kbuf[tb:tb + WINDOW, :]
    vt = vbuf[tb:tb + WINDOW, :]
    kbuf[0:WINDOW, :] = kt
    vbuf[0:WINDOW, :] = vt
    kwin_ref[...] = kt
    vwin_ref[...] = vt
    for hh in range(B_HEADS):
        sout_ref[hh] = st_s[hh].T


def _layer_call(x, cache, weights, pos0, tb, n_sorted_tiles=None):
    bsz, seq, _ = x.shape
    nblk = seq // tb
    sorted_out = cache is None
    g1, w_in, sink, lower_bounds, hg, w_o, g2, w_r, b_r = weights

    def const(shape):
        return pl.BlockSpec(shape, lambda b, j: (0,) * len(shape))

    def per_stream(shape):
        return pl.BlockSpec((None,) + shape, lambda b, j: (b,) + (0,) * len(shape))

    in_specs = [pl.BlockSpec((None, tb, D_MODEL), lambda b, j: (b, j, 0))]
    args = [x]
    if not sorted_out:
        in_specs += [per_stream((WINDOW, A_KV_WIDTH)), per_stream((WINDOW, A_KV_WIDTH)),
                     per_stream((B_HEADS, B_KEY_DIM, B_VAL_DIM))]
        args += list(cache)
    in_specs += [const((1, D_MODEL)), const((D_MODEL, N_IN)),
                 pl.BlockSpec(memory_space=pltpu.SMEM),
                 const((2, B_KEY_WIDTH)), const((1, B_WIDTH)), const((MIX_WIDTH, D_MODEL)),
                 const((1, D_MODEL)), const((D_MODEL, LANES)), const((1, LANES))]
    args += [g1, w_in, sink, lower_bounds, hg, w_o, g2, w_r, b_r]
    stream_shapes = (jax.ShapeDtypeStruct((bsz, WINDOW, A_KV_WIDTH), F32),
                     jax.ShapeDtypeStruct((bsz, WINDOW, A_KV_WIDTH), F32),
                     jax.ShapeDtypeStruct((bsz, B_HEADS, B_KEY_DIM, B_VAL_DIM), F32))
    stream_specs = (per_stream((WINDOW, A_KV_WIDTH)), per_stream((WINDOW, A_KV_WIDTH)),
                    per_stream((B_HEADS, B_KEY_DIM, B_VAL_DIM)))
    scratch = [pltpu.VMEM((tb, N_IN), F32),
               pltpu.VMEM((WINDOW + tb, A_KV_WIDTH), F32),
               pltpu.VMEM((WINDOW + tb, A_KV_WIDTH), F32),
               pltpu.VMEM((B_HEADS, B_VAL_DIM, B_KEY_DIM), F32),
               pltpu.VMEM((tb, MIX_WIDTH), BF16)]
    if sorted_out:
        out_shape = (jax.ShapeDtypeStruct((n_sorted_tiles * MOE_TILE, ROW_WIDTH), F32),
                     jax.ShapeDtypeStruct((SUBLANES, TILE_LANES), I32),
                     jax.ShapeDtypeStruct((CLS_ROWS, LANES), F32)) + stream_shapes
        out_specs = (pl.BlockSpec(memory_space=pl.ANY), const((SUBLANES, TILE_LANES)),
                     const((CLS_ROWS, LANES))) + stream_specs
        scratch += [pltpu.VMEM((2, tb, ROW_WIDTH), F32),
                    pltpu.VMEM((CLS_ROWS, LANES), F32),
                    pltpu.VMEM((CLS_ROWS, LANES), F32),
                    pltpu.VMEM((SUBLANES, TILE_LANES), I32),
                    pltpu.VMEM((SUBLANES, tb), I32),
                    pltpu.SMEM((2, tb), I32),
                    pltpu.SemaphoreType.DMA(()),
                    pltpu.SemaphoreType.DMA((2,))]
    else:
        out_shape = (jax.ShapeDtypeStruct((bsz, seq, D_MODEL), F32),
                     jax.ShapeDtypeStruct((bsz, seq, LANES), F32)) + stream_shapes
        out_specs = (pl.BlockSpec((None, tb, D_MODEL), lambda b, j: (b, j, 0)),
                     pl.BlockSpec((None, tb, LANES), lambda b, j: (b, j, 0))) + stream_specs
    return pl.pallas_call(
        functools.partial(_layer_body, pos0, tb, sorted_out),
        grid=(bsz, nblk),
        in_specs=in_specs,
        out_specs=out_specs,
        out_shape=out_shape,
        scratch_shapes=scratch,
        compiler_params=pltpu.CompilerParams(
            dimension_semantics=("arbitrary", "arbitrary"),
            vmem_limit_bytes=VMEM_LIMIT_BYTES),
        name="layer_prompt" if sorted_out else "layer_sample",
    )(*args)


def _moe_sorted_body(blk_ref, ea_ref, eb_ref, nv_ref, nused_ref, xs_ref, g2_ref,
                     wga_ref, wua_ref, wda_ref, wgb_ref, wub_ref, wdb_ref, gf_ref, y_ref,
                     ybuf, h2_s, ya_s, tok_v, tok_sm, tsem, rsem):
    p = pl.program_id(0)
    n_used = nused_ref[0]
    slot = p % 2
    prev = 1 - slot
    nv_prev = nv_ref[jnp.maximum(p - 1, 0)]
    nv_prev2 = nv_ref[jnp.maximum(p - 2, 0)]

    def expert(h2, wg_ref, wu_ref, wd_ref):
        a = jnp.dot(h2, wg_ref[...], preferred_element_type=F32)
        u = jnp.dot(h2, wu_ref[...], preferred_element_type=F32)
        return jnp.dot((_silu(a) * u).astype(BF16), wd_ref[...], preferred_element_type=F32)

    def tok_copy():
        return pltpu.make_async_copy(tok_v.at[pl.ds(0, 1)], tok_sm.at[pl.ds(slot, 1)], tsem)

    def first_half():
        meta = xs_ref[:, D_MODEL:ROW_WIDTH]
        sr = lax.broadcasted_iota(I32, (SUBLANES, LANES), 0)
        sc = lax.broadcasted_iota(I32, (SUBLANES, LANES), 1)
        pick = jnp.where(sc == sr + META_TOK_HI, 1.0, 0.0).astype(BF16)
        ids = lax.dot_general(pick, meta.astype(BF16), NT_DIMS, preferred_element_type=F32)
        tok = (ids[0:1, :] * TOK_SPLIT + ids[1:2, :]).astype(I32)
        tok_v[...] = jnp.broadcast_to(tok, tok_v.shape)
        tok_copy().start()
        h2 = _rmsnorm(xs_ref[:, 0:D_MODEL], g2_ref[...]).astype(BF16)
        h2_s[...] = h2
        ya_s[...] = expert(h2, wga_ref, wua_ref, wda_ref)

    def second_half():
        x1 = xs_ref[:, 0:D_MODEL]
        meta = xs_ref[:, D_MODEL:ROW_WIDTH]
        yb = expert(h2_s[...], wgb_ref, wub_ref, wdb_ref)
        moe = meta[:, META_GA:META_GA + 1] * ya_s[...] + meta[:, META_GB:META_GB + 1] * yb
        ybuf[slot] = _rmsnorm(x1 + moe, gf_ref[...])
        tok_copy().wait()

    def send_full():
        _issue_rows(MOE_TILE // 2, ybuf, prev, y_ref, tok_sm, rsem)

    def send_full_rest():
        _issue_rows(MOE_TILE, ybuf, prev, y_ref, tok_sm, rsem, first=MOE_TILE // 2)

    def send_part():
        def start(r, carry):
            _row_copy(ybuf.at[prev], r, y_ref, tok_sm[prev, r], rsem.at[prev]).start()
            return carry

        lax.fori_loop(0, nv_prev, start, 0)

    def wait_full():
        pltpu.make_async_copy(ybuf.at[slot], y_ref.at[pl.ds(0, MOE_TILE)], rsem.at[slot]).wait()

    def wait_part():
        def wait(r, carry):
            _row_copy(ybuf.at[slot], r, y_ref, 0, rsem.at[slot]).wait()
            return carry

        lax.fori_loop(0, nv_prev2, wait, 0)

    has_prev = (p >= 1) & (p <= n_used)
    prev_full = nv_prev == MOE_TILE
    prev_part = jnp.logical_not(prev_full)
    has_prev2 = (p >= 2) & (p <= n_used + 1)
    prev2_full = nv_prev2 == MOE_TILE
    active = p < n_used
    idle = jnp.logical_not(active)
    steady = active & has_prev & prev_full

    @pl.when(steady)
    def _steady_first():
        send_full()
        first_half()

    @pl.when(active & has_prev & prev_part)
    def _after_partial():
        send_part()
        first_half()

    @pl.when(active & jnp.logical_not(has_prev))
    def _first():
        first_half()

    @pl.when(idle & has_prev & prev_full)
    def _flush_full():
        send_full()
        send_full_rest()

    @pl.when(idle & has_prev & prev_part)
    def _flush_part():
        send_part()

    @pl.when(has_prev2 & prev2_full)
    def _wait_full():
        wait_full()

    @pl.when(has_prev2 & jnp.logical_not(prev2_full))
    def _wait_part():
        wait_part()

    @pl.when(steady)
    def _steady_second():
        send_full_rest()
        second_half()

    @pl.when(active & jnp.logical_not(steady))
    def _second():
        second_half()


def _moe_sorted_call(step_blk, step_ea, step_eb, step_nv, n_used, xs, g2, wg, wu, wd, gf, n_tok):
    n_steps = step_blk.shape[0]

    def const(shape):
        return pl.BlockSpec(shape, lambda p, blk, ea, eb, nv, nu: (0,) * len(shape))

    def w_a(shape):
        return pl.BlockSpec((None,) + shape, lambda p, blk, ea, eb, nv, nu: (ea[p], 0, 0))

    def w_b(shape):
        return pl.BlockSpec((None,) + shape, lambda p, blk, ea, eb, nv, nu: (eb[p], 0, 0))

    return pl.pallas_call(
        _moe_sorted_body,
        grid_spec=pltpu.PrefetchScalarGridSpec(
            num_scalar_prefetch=5,
            grid=(n_steps,),
            in_specs=[pl.BlockSpec((MOE_TILE, ROW_WIDTH), lambda p, blk, ea, eb, nv, nu: (blk[p], 0)),
                      const((1, D_MODEL)),
                      w_a((D_MODEL, D_EXPERT)), w_a((D_MODEL, D_EXPERT)), w_a((D_EXPERT, D_MODEL)),
                      w_b((D_MODEL, D_EXPERT)), w_b((D_MODEL, D_EXPERT)), w_b((D_EXPERT, D_MODEL)),
                      const((1, D_MODEL))],
            out_specs=pl.BlockSpec(memory_space=pl.ANY),
            scratch_shapes=[pltpu.VMEM((2, MOE_TILE, D_MODEL), F32),
                            pltpu.VMEM((MOE_TILE, D_MODEL), BF16),
                            pltpu.VMEM((MOE_TILE, D_MODEL), F32),
                            pltpu.VMEM((SUBLANES, MOE_TILE), I32),
                            pltpu.SMEM((2, MOE_TILE), I32),
                            pltpu.SemaphoreType.DMA(()),
                            pltpu.SemaphoreType.DMA((2,))]),
        out_shape=jax.ShapeDtypeStruct((n_tok, D_MODEL), F32),
        compiler_params=pltpu.CompilerParams(
            dimension_semantics=("arbitrary",), vmem_limit_bytes=VMEM_LIMIT_BYTES),
        name="moe_sorted",
    )(step_blk, step_ea, step_eb, step_nv, n_used, xs, g2, wg, wu, wd, wg, wu, wd, gf)


def _moe_dense_body(x1_ref, route_ref, g2_ref, wg_ref, wu_ref, wd_ref, gf_ref, y_ref, h2_s, acc_s):
    e = pl.program_id(1)

    @pl.when(e == 0)
    def _first():
        h2_s[...] = _rmsnorm(x1_ref[...], g2_ref[...]).astype(BF16)
        acc_s[...] = jnp.zeros(acc_s.shape, F32)

    h2 = h2_s[...]
    a = jnp.dot(h2, wg_ref[...], preferred_element_type=F32)
    u = jnp.dot(h2, wu_ref[...], preferred_element_type=F32)
    y = jnp.dot((_silu(a) * u).astype(BF16), wd_ref[...], preferred_element_type=F32)
    route = route_ref[...]
    lane = lax.broadcasted_iota(I32, route.shape, 1)
    gate = jnp.sum(jnp.where(lane == e + ROUTE_E0, route, 0.0), axis=-1, keepdims=True)
    acc_s[...] += gate * y

    @pl.when(e == N_EXPERTS - 1)
    def _last():
        y_ref[...] = _rmsnorm(x1_ref[...] + acc_s[...], gf_ref[...])


def _moe_dense_call(x1, route, g2, wg, wu, wd, gf, tm):
    n = x1.shape[0]
    return pl.pallas_call(
        _moe_dense_body,
        grid=(n // tm, N_EXPERTS),
        in_specs=[pl.BlockSpec((tm, D_MODEL), lambda i, e: (i, 0)),
                  pl.BlockSpec((tm, LANES), lambda i, e: (i, 0)),
                  pl.BlockSpec((1, D_MODEL), lambda i, e: (0, 0)),
                  pl.BlockSpec((None, D_MODEL, D_EXPERT), lambda i, e: (e, 0, 0)),
                  pl.BlockSpec((None, D_MODEL, D_EXPERT), lambda i, e: (e, 0, 0)),
                  pl.BlockSpec((None, D_EXPERT, D_MODEL), lambda i, e: (e, 0, 0)),
                  pl.BlockSpec((1, D_MODEL), lambda i, e: (0, 0))],
        out_specs=pl.BlockSpec((tm, D_MODEL), lambda i, e: (i, 0)),
        out_shape=jax.ShapeDtypeStruct((n, D_MODEL), F32),
        scratch_shapes=[pltpu.VMEM((tm, D_MODEL), BF16), pltpu.VMEM((tm, D_MODEL), F32)],
        compiler_params=pltpu.CompilerParams(
            dimension_semantics=("arbitrary", "arbitrary"),
            vmem_limit_bytes=VMEM_LIMIT_BYTES),
        name="moe_dense",
    )(x1, route, g2, wg, wu, wd, gf)


def _pick_block(seq, target):
    tb = min(seq, target)
    assert seq % tb == 0 and tb % CHUNK == 0
    return tb


def _lookup(table, idx):
    table = jnp.asarray(table, I32)
    k = jnp.arange(table.shape[0], dtype=I32)
    return jnp.sum(jnp.where(idx[..., None] == k, table, 0), axis=-1).astype(I32)


def _plan_steps(tab, counts, n_tiles):
    tile = jnp.arange(n_tiles, dtype=I32)
    n_used = tab[TAB_USED, 0]
    t_cls = tab[TAB_CLS, :n_tiles]
    t_ord = tab[TAB_ORD, :n_tiles]
    live = tile < n_used
    key = jnp.where(live, t_cls * n_tiles + tile, N_CLASSES * n_tiles + tile)
    pos = jnp.sum(key[None, :] < key[:, None], axis=1).astype(I32)
    step = jnp.arange(n_tiles + 2, dtype=I32)
    step_c = jnp.minimum(step, n_used - 1)
    step_tile = jnp.sum(jnp.where(pos[None, :] == step_c[:, None], tile[None, :], 0), axis=1).astype(I32)
    s_cls = _lookup(t_cls, step_tile)
    s_ord = _lookup(t_ord, step_tile)
    s_nv = jnp.clip(_lookup(counts, s_cls) - s_ord * MOE_TILE, 0, MOE_TILE).astype(I32)
    cls_ea = np.array([g * EXPERTS_PER_GROUP + a for g in range(N_GROUPS) for a, _ in PAIRS], np.int32)
    cls_eb = np.array([g * EXPERTS_PER_GROUP + b for g in range(N_GROUPS) for _, b in PAIRS], np.int32)
    return step_tile, _lookup(cls_ea, s_cls), _lookup(cls_eb, s_cls), s_nv, n_used.reshape(1).astype(I32)


def kernel(x_prompt, x_sample, cache_k, cache_v, state_hgrn, norm1_g, w_in, attn_sink, lower_bounds,
           hgrn_norm_g, w_o, norm2_g, w_router_group, b_router_group, w_router_expert, b_router_expert,
           w_gate, w_up, w_down, final_norm_g):
    depth = w_in.shape[0]
    assert depth == 1
    l = 0
    w_hist = cache_k.shape[2]
    assert w_hist == WINDOW
    w_r = jnp.concatenate(
        [w_router_group[l], jnp.transpose(w_router_expert[l], (1, 0, 2)).reshape(D_MODEL, N_EXPERTS)], axis=1)
    w_r = jnp.pad(w_r, ((0, 0), (0, LANES - w_r.shape[1]))).astype(BF16)
    b_r = jnp.concatenate([b_router_group[l], b_router_expert[l].reshape(N_EXPERTS)])
    b_r = jnp.pad(b_r, (0, LANES - b_r.shape[0])).reshape(1, LANES).astype(F32)
    weights = (norm1_g[l].reshape(1, D_MODEL), w_in[l].astype(BF16), attn_sink[l].astype(F32),
               lower_bounds.astype(F32), hgrn_norm_g[l].reshape(1, B_WIDTH), w_o[l].astype(BF16),
               norm2_g[l].reshape(1, D_MODEL), w_r, b_r)
    g2 = norm2_g[l].reshape(1, D_MODEL)
    gf = final_norm_g.reshape(1, D_MODEL)
    wg, wu, wd = w_gate[l].astype(BF16), w_up[l].astype(BF16), w_down[l].astype(BF16)

    bp, lp, _ = x_prompt.shape
    bs, ls, _ = x_sample.shape
    np_tok, ns_tok = bp * lp, bs * ls
    tbp, tbs = _pick_block(lp, LAYER_BLOCK), _pick_block(ls, LAYER_BLOCK)
    assert tbp % MOE_TILE == 0 and np_tok % MOE_TILE == 0 and np_tok < TOK_SPLIT * TOK_SPLIT

    n_tiles = np_tok // MOE_TILE + N_CLASSES
    assert n_tiles <= TILE_LANES
    xs, tab, cnt, kp, vp, sp = _layer_call(x_prompt, None, weights, 0, tbp, n_tiles + tbp // MOE_TILE)
    counts = cnt[:N_CLASSES, 0].astype(I32)
    step_blk, step_ea, step_eb, step_nv, n_used = _plan_steps(tab, counts, n_tiles)
    yp = _moe_sorted_call(step_blk, step_ea, step_eb, step_nv, n_used, xs, g2, wg, wu, wd, gf, np_tok)

    cache = (cache_k[l].reshape(bs, w_hist, A_KV_WIDTH), cache_v[l].reshape(bs, w_hist, A_KV_WIDTH), state_hgrn[l])
    x1s, routes, kn, vn, sn = _layer_call(x_sample, cache, weights, PAST_LEN, tbs)
    ys = _moe_dense_call(x1s.reshape(ns_tok, D_MODEL), routes.reshape(ns_tok, LANES), g2, wg, wu, wd, gf,
                         _pick_block(ns_tok, LAYER_BLOCK))

    kv_shape = (1, -1, w_hist, A_KV_HEADS, A_HEAD_DIM)
    return (yp.reshape(bp, lp, D_MODEL), ys.reshape(bs, ls, D_MODEL),
            kp.reshape(kv_shape), vp.reshape(kv_shape), sp[None],
            kn.reshape(kv_shape), vn.reshape(kv_shape), sn[None])
```

```python
import functools

import numpy as np
import jax
import jax.numpy as jnp
from jax import lax
from jax.experimental import pallas as pl
from jax.experimental.pallas import tpu as pltpu

F32 = jnp.float32
BF16 = jnp.bfloat16
I32 = jnp.int32

D_MODEL = 1024
CHUNK = 64
EPS = 1e-6
PAST_LEN = 4096
WINDOW = 128
A_HEADS = 8
A_KV_HEADS = 2
A_HEAD_DIM = 64
A_GROUP = A_HEADS // A_KV_HEADS
A_WIDTH = A_HEADS * A_HEAD_DIM
A_KV_WIDTH = A_KV_HEADS * A_HEAD_DIM
B_HEADS = 4
B_KEY_DIM = 128
B_VAL_DIM = 128
B_KEY_WIDTH = B_HEADS * B_KEY_DIM
B_WIDTH = B_HEADS * B_VAL_DIM
MIX_WIDTH = A_WIDTH + B_WIDTH
OFF_K = A_WIDTH
OFF_V = OFF_K + A_KV_WIDTH
OFF_QB = OFF_V + A_KV_WIDTH
OFF_FB = OFF_QB + B_KEY_WIDTH
OFF_IB = OFF_FB + B_KEY_WIDTH
OFF_GB = OFF_IB + B_WIDTH
N_IN = OFF_GB + B_WIDTH
N_GROUPS = 4
EXPERTS_PER_GROUP = 4
N_EXPERTS = N_GROUPS * EXPERTS_PER_GROUP
D_EXPERT = 256
LANES = 128
SUBLANES = 8
ROUTE_E0 = N_GROUPS
PAIRS = [(a, b) for a in range(EXPERTS_PER_GROUP) for b in range(a + 1, EXPERTS_PER_GROUP)]
N_PAIRS = len(PAIRS)
N_CLASSES = N_GROUPS * N_PAIRS
CLS_ROWS = 32
ROW_WIDTH = D_MODEL + LANES
META_GA, META_GB, META_TOK_HI, META_TOK_LO = 0, 1, 2, 3
TOK_SPLIT = 256
LAYER_BLOCK = 512
MOE_TILE = 256
MOE_TILE_SHIFT = 8
TILE_LANES = 256
TAB_CLS, TAB_ORD, TAB_USED = 0, 1, 2

VMEM_LIMIT_BYTES = 56 * 1024 * 1024

NT_DIMS = (((1,), (1,)), ((), ()))
TN_DIMS = (((0,), (0,)), ((), ()))


def _rmsnorm(x, g):
    return x * lax.rsqrt(jnp.mean(x * x, axis=-1, keepdims=True) + EPS) * g


def _silu(x):
    hx = 0.5 * x
    return hx + hx * jnp.tanh(hx)


def _route(logits):
    lane = lax.broadcasted_iota(I32, logits.shape, 1)
    neg = -jnp.inf
    is_g = lane < N_GROUPS
    gl = jnp.where(is_g, logits, neg)
    gmax = jnp.max(gl, axis=-1, keepdims=True)
    gidx = jnp.min(jnp.where(gl == gmax, lane, LANES), axis=-1, keepdims=True)
    p_group = 1.0 / jnp.sum(jnp.where(is_g, jnp.exp(logits - gmax), 0.0), axis=-1, keepdims=True)
    e_lo = ROUTE_E0 + EXPERTS_PER_GROUP * gidx
    in_group = (lane >= e_lo) & (lane < e_lo + EXPERTS_PER_GROUP)
    el = jnp.where(in_group, logits, neg)
    e1 = jnp.max(el, axis=-1, keepdims=True)
    i1 = jnp.min(jnp.where(el == e1, lane, LANES), axis=-1, keepdims=True)
    el2 = jnp.where(lane == i1, neg, el)
    e2 = jnp.max(el2, axis=-1, keepdims=True)
    i2 = jnp.min(jnp.where(el2 == e2, lane, LANES), axis=-1, keepdims=True)
    t = jnp.exp(e2 - e1)
    w1 = p_group / (1.0 + t)
    w2 = w1 * t
    first_low = i1 < i2
    ea = jnp.where(first_low, i1, i2) - e_lo
    eb = jnp.where(first_low, i2, i1) - e_lo
    ga = jnp.where(first_low, w1, w2)
    gb = jnp.where(first_low, w2, w1)
    pair = ((ea * (2 * EXPERTS_PER_GROUP - 1 - ea)) >> 1) + eb - ea - 1
    dense = jnp.where(lane == i1, w1, 0.0) + jnp.where(lane == i2, w2, 0.0)
    return gidx * N_PAIRS + pair, ga, gb, dense


def _mixer_and_router(pos0, tb, j, x_ref, g1_ref, win_ref, sink_ref, lb_ref, hg_ref, wo_ref, g2_ref, wr_ref, br_ref,
                      proj_s, kbuf, vbuf, st_s, mix_s, sc_s, cum_s, kf_s, qf_s, qcum_s, dec_s, a_s, ds_s):
    nc = tb // CHUNK
    nk = WINDOW + CHUNK
    x = x_ref[...]
    h = _rmsnorm(x, g1_ref[...]).astype(BF16)
    proj_s[...] = jnp.dot(h, win_ref[...], preferred_element_type=F32)
    kbuf[WINDOW:WINDOW + tb, :] = proj_s[:, OFF_K:OFF_K + A_KV_WIDTH]
    vbuf[WINDOW:WINDOW + tb, :] = proj_s[:, OFF_V:OFF_V + A_KV_WIDTH]

    row = lax.broadcasted_iota(I32, (A_GROUP * CHUNK, nk), 0)
    col = lax.broadcasted_iota(I32, (A_GROUP * CHUNK, nk), 1)
    dist = jnp.abs((row & (CHUNK - 1)) - (col - WINDOW)).astype(F32)
    row_head = row // CHUNK
    rowc_head = lax.broadcasted_iota(I32, (A_GROUP * CHUNK, 1), 0) // CHUNK
    bias = []
    sinkc = []
    for hk in range(A_KV_HEADS):
        slope = jnp.exp2(-(row_head + (hk * A_GROUP + 1)).astype(F32))
        bias.append(slope * dist)
        sc = jnp.zeros((A_GROUP * CHUNK, 1), F32)
        for g in range(A_GROUP):
            sc = jnp.where(rowc_head == g, sink_ref[hk * A_GROUP + g], sc)
        sinkc.append(sc)

    lbr = lb_ref[...]
    lbm = jnp.max(lbr, axis=0, keepdims=True)
    lbe = jnp.exp(lbr - lbm)
    lb = lbe[0:1, :] / jnp.sum(lbe, axis=0, keepdims=True)
    f_mid = 0.5 * (1.0 + lb)
    f_half = 0.5 * (1.0 - lb)
    hg = hg_ref[...]
    tr = lax.broadcasted_iota(I32, (CHUNK, CHUNK), 0)
    tc = lax.broadcasted_iota(I32, (CHUNK, CHUNK), 1)
    tril = tr >= tc
    ltri = jnp.where(tril, 1.0, 0.0).astype(BF16)

    def chunk_rows(c):
        r0 = pl.multiple_of(c * CHUNK, CHUNK)
        return r0, pl.ds(r0, CHUNK)

    def stage_scores(c, par):
        r0, rows = chunk_rows(c)
        for hk in range(A_KV_HEADS):
            qc = proj_s[rows, hk * A_GROUP * A_HEAD_DIM:(hk + 1) * A_GROUP * A_HEAD_DIM] * (A_HEAD_DIM ** -0.5)
            q4 = jnp.concatenate([qc[:, g * A_HEAD_DIM:(g + 1) * A_HEAD_DIM] for g in range(A_GROUP)],
                                 axis=0).astype(BF16)
            kw = kbuf[pl.ds(r0, nk), hk * A_HEAD_DIM:(hk + 1) * A_HEAD_DIM].astype(BF16)
            sc_s[par * A_KV_HEADS + hk] = lax.dot_general(q4, kw, NT_DIMS, preferred_element_type=F32)
        qb = proj_s[rows, OFF_QB:OFF_QB + B_KEY_WIDTH]
        fl = proj_s[rows, OFF_FB:OFF_FB + B_KEY_WIDTH]
        f = f_mid + f_half * jnp.tanh(0.5 * fl)
        logf = jnp.log(f)
        hi = logf.astype(BF16)
        lo = (logf - hi.astype(F32)).astype(BF16)
        cum_s[par] = (jnp.dot(ltri, hi, preferred_element_type=F32)
                      + jnp.dot(ltri, lo, preferred_element_type=F32))
        kf_s[par] = 1.0 - f
        qf_s[par] = _silu(qb)

    def stage_softmax(c, par):
        r0, rows = chunk_rows(c)
        valid = col >= (WINDOW - pos0) - (j * tb + c * CHUNK)
        for hk in range(A_KV_HEADS):
            vw = vbuf[pl.ds(r0, nk), hk * A_HEAD_DIM:(hk + 1) * A_HEAD_DIM].astype(BF16)
            s = jnp.where(valid, sc_s[par * A_KV_HEADS + hk] - bias[hk], -jnp.inf)
            m = jnp.maximum(jnp.max(s, axis=-1, keepdims=True), sinkc[hk])
            p = jnp.exp(s - m)
            den = jnp.sum(p, axis=-1, keepdims=True) + jnp.exp(sinkc[hk] - m)
            o = jnp.dot(p.astype(BF16), vw, preferred_element_type=F32) / den
            att = jnp.concatenate([o[g * CHUNK:(g + 1) * CHUNK] for g in range(A_GROUP)], axis=1)
            mix_s[rows, hk * A_GROUP * A_HEAD_DIM:(hk + 1) * A_GROUP * A_HEAD_DIM] = att.astype(BF16)
        cum = cum_s[par]
        qf = qf_s[par]
        kf = kf_s[par]
        vb = proj_s[rows, OFF_IB:OFF_IB + B_WIDTH].astype(BF16)
        ref = cum[CHUNK // 2:CHUNK // 2 + 1, :]
        tot = cum[CHUNK - 1:CHUNK, :]
        e_fwd = jnp.exp(cum - ref)
        e_bwd = 1.0 / e_fwd
        q_rel = qf * e_fwd
        k_rel = kf * e_bwd
        q_in = q_rel.astype(BF16)
        k_in = k_rel.astype(BF16)
        k_tot = (k_rel * jnp.exp(tot - ref)).astype(BF16)
        qcum_s[par] = (q_rel * jnp.exp(ref)).astype(BF16)
        dec_s[par] = jnp.exp(tot)
        for hh in range(B_HEADS):
            sl = slice(hh * B_KEY_DIM, (hh + 1) * B_KEY_DIM)
            a = lax.dot_general(q_in[:, sl], k_in[:, sl], NT_DIMS, preferred_element_type=F32)
            a_s[par * B_HEADS + hh] = jnp.where(tril, a, 0.0).astype(BF16)
            ds_s[par * B_HEADS + hh] = lax.dot_general(vb[:, sl], k_tot[:, sl], TN_DIMS,
                                                       preferred_element_type=F32)

    def stage_state(c, par):
        r0, rows = chunk_rows(c)
        vb = proj_s[rows, OFF_IB:OFF_IB + B_WIDTH].astype(BF16)
        gg = proj_s[rows, OFF_GB:OFF_GB + B_WIDTH]
        dec = dec_s[par]
        outs = []
        for hh in range(B_HEADS):
            sl = slice(hh * B_KEY_DIM, (hh + 1) * B_KEY_DIM)
            st = st_s[hh]
            o = (jnp.dot(a_s[par * B_HEADS + hh], vb[:, sl], preferred_element_type=F32)
                 + lax.dot_general(qcum_s[par, :, sl], st.astype(BF16), NT_DIMS, preferred_element_type=F32))
            st_s[hh] = st * dec[:, sl] + ds_s[par * B_HEADS + hh]
            o = o * lax.rsqrt(jnp.mean(o * o, axis=-1, keepdims=True) + EPS) * hg[:, sl]
            outs.append(o)
        rec = jnp.concatenate(outs, axis=1) * _silu(gg)
        mix_s[rows, A_WIDTH:A_WIDTH + B_WIDTH] = rec.astype(BF16)

    if nc >= 4 and nc % 2 == 0:
        stage_scores(0, 0)
        stage_scores(1, 1)
        stage_softmax(0, 0)

        def pipelined(k, carry):
            for u in range(2):
                i = 2 * k + u
                stage_scores(i + 2, u)
                stage_softmax(i + 1, 1 - u)
                stage_state(i, u)
            return carry

        lax.fori_loop(0, (nc - 2) // 2, pipelined, 0)
        stage_softmax(nc - 1, 1)
        stage_state(nc - 2, 0)
        stage_state(nc - 1, 1)
    else:
        for c in range(nc):
            stage_scores(c, c % 2)
            stage_softmax(c, c % 2)
            stage_state(c, c % 2)

    x1 = x + jnp.dot(mix_s[...], wo_ref[...], preferred_element_type=F32)
    h2 = _rmsnorm(x1, g2_ref[...]).astype(BF16)
    logits = jnp.dot(h2, wr_ref[...], preferred_element_type=F32) + br_ref[...]
    return x1, logits


def _row_copy(src_ref, src_row, dst_ref, dst_row, sem):
    return pltpu.make_async_copy(src_ref.at[pl.ds(src_row, 1)], dst_ref.at[pl.ds(dst_row, 1)], sem)


def _issue_rows(n, src_ref, src_slot, dst_ref, idx_ref, sem, first=0):
    for r in range(first, n):
        _row_copy(src_ref.at[src_slot], r, dst_ref, idx_ref[src_slot, r], sem.at[src_slot]).start(priority=r % 2)


def _layer_body(pos0, tb, sorted_out, *refs):
    if sorted_out:
        (x_ref, g1_ref, win_ref, sink_ref, lb_ref, hg_ref, wo_ref, g2_ref, wr_ref, br_ref,
         xs_ref, tab_ref, cnt_ref, kwin_ref, vwin_ref, sout_ref,
         proj_s, kbuf, vbuf, st_s, mix_s, *pipe_s, xrow_s, cnt_s, cur_s, tab_s, dest_v, dest_sm, dsem, rsem) = refs
    else:
        (x_ref, ck_ref, cv_ref, s0_ref, g1_ref, win_ref, sink_ref, lb_ref, hg_ref, wo_ref, g2_ref, wr_ref, br_ref,
         x1_ref, route_ref, kwin_ref, vwin_ref, sout_ref,
         proj_s, kbuf, vbuf, st_s, mix_s, *pipe_s) = refs
    b = pl.program_id(0)
    j = pl.program_id(1)
    nblk = pl.num_programs(1)
    step = b * nblk + j

    if sorted_out:
        slot = step % 2
        prev = 1 - slot
        n_tiles = xs_ref.shape[0] // MOE_TILE
        spare_row0 = (n_tiles - tb // MOE_TILE) * MOE_TILE

        @pl.when(step == 0)
        def _init_routing():
            cnt_s[...] = jnp.zeros(cnt_s.shape, F32)
            cur_s[...] = jnp.zeros(cur_s.shape, F32)
            tab_s[...] = jnp.zeros(tab_s.shape, I32)
            xrow_s[1] = jnp.zeros(xrow_s.shape[1:], F32)

            def fill(r, carry):
                dest_sm[1, r] = spare_row0 + r
                return carry

            lax.fori_loop(0, tb, fill, 0)

    @pl.when(j == 0)
    def _init_stream():
        if sorted_out:
            kbuf[0:WINDOW, :] = jnp.zeros((WINDOW, A_KV_WIDTH), F32)
            vbuf[0:WINDOW, :] = jnp.zeros((WINDOW, A_KV_WIDTH), F32)
            st_s[...] = jnp.zeros(st_s.shape, F32)
        else:
            kbuf[0:WINDOW, :] = ck_ref[...]
            vbuf[0:WINDOW, :] = cv_ref[...]
            for hh in range(B_HEADS):
                st_s[hh] = s0_ref[hh].T

    if sorted_out:
        _issue_rows(tb, xrow_s, prev, xs_ref, dest_sm, rsem)

    x1, logits = _mixer_and_router(pos0, tb, j, x_ref, g1_ref, win_ref, sink_ref, lb_ref, hg_ref, wo_ref, g2_ref,
                                   wr_ref, br_ref, proj_s, kbuf, vbuf, st_s, mix_s, *pipe_s)
    cls, ga, gb, dense = _route(logits)

    if not sorted_out:
        x1_ref[...] = x1
        route_ref[...] = dense
    else:
        lane = lax.broadcasted_iota(I32, (tb, LANES), 1)
        tok = step * tb + lax.broadcasted_iota(I32, (tb, 1), 0)
        tok_hi = (tok >> 8).astype(F32)
        tok_lo = (tok & (TOK_SPLIT - 1)).astype(F32)
        meta = jnp.where(lane == META_GA, ga,
                         jnp.where(lane == META_GB, gb,
                                   jnp.where(lane == META_TOK_HI, tok_hi,
                                             jnp.where(lane == META_TOK_LO, tok_lo, 0.0))))
        xrow_s[slot, :, 0:D_MODEL] = x1
        xrow_s[slot, :, D_MODEL:ROW_WIDTH] = meta

        onehot = jnp.where(lane == cls, 1.0, 0.0).astype(BF16)
        er = lax.broadcasted_iota(I32, (CLS_ROWS, LANES), 0)
        ec = lax.broadcasted_iota(I32, (CLS_ROWS, LANES), 1)
        eye = jnp.where(er == ec, 1.0, 0.0).astype(BF16)
        oht = lax.dot_general(eye, onehot, NT_DIMS, preferred_element_type=F32)
        ur = lax.broadcasted_iota(I32, (tb, tb), 0)
        uc = lax.broadcasted_iota(I32, (tb, tb), 1)
        before = jnp.where(ur < uc, 1.0, 0.0).astype(BF16)
        prefix = jnp.dot(oht.astype(BF16), before, preferred_element_type=F32)
        cnt = cnt_s[:, 0:1]
        rank = jnp.sum(oht * (prefix + cnt), axis=0, keepdims=True).astype(I32)

        n_c = jnp.sum(oht, axis=1, keepdims=True)
        cnt_i = cnt.astype(I32)
        after_i = (cnt + n_c).astype(I32)
        tiles_before = (cnt_i + (MOE_TILE - 1)) >> MOE_TILE_SHIFT
        new_c = ((after_i + (MOE_TILE - 1)) >> MOE_TILE_SHIFT) - tiles_before
        sr = lax.broadcasted_iota(I32, (CLS_ROWS, CLS_ROWS), 0)
        sc = lax.broadcasted_iota(I32, (CLS_ROWS, CLS_ROWS), 1)
        lower = jnp.where(sr > sc, 1.0, 0.0).astype(BF16)
        new_b = jnp.broadcast_to(new_c.astype(F32), (CLS_ROWS, LANES)).astype(BF16)
        opened_before = jnp.dot(lower, new_b, preferred_element_type=F32)[:, 0:1].astype(I32)
        used = tab_s[TAB_USED:TAB_USED + 1, 0:1]
        base_c = used + opened_before
        partial = (cnt_i & (MOE_TILE - 1)) != 0
        cur_c = cur_s[:, 0:1].astype(I32)
        first_c = jnp.where(partial, cur_c, base_c)
        shift_c = jnp.where(partial, base_c - 1, base_c)
        ord0_c = cnt_i >> MOE_TILE_SHIFT

        def per_token(col):
            return jnp.sum(oht * col.astype(F32), axis=0, keepdims=True).astype(I32)

        ord_t = rank >> MOE_TILE_SHIFT
        d_t = ord_t - per_token(ord0_c)
        tile_t = jnp.where(d_t == 0, per_token(first_c), per_token(shift_c) + d_t)
        dest = tile_t * MOE_TILE + (rank & (MOE_TILE - 1))

        tl = lax.broadcasted_iota(I32, (CLS_ROWS, TILE_LANES), 1)
        opened = (tl >= base_c) & (tl < base_c + new_c)
        cls_id = lax.broadcasted_iota(I32, (CLS_ROWS, TILE_LANES), 0)
        hit = jnp.sum(jnp.where(opened, 1, 0), axis=0, keepdims=True) > 0
        t_cls = jnp.sum(jnp.where(opened, cls_id, 0), axis=0, keepdims=True)
        t_ord = jnp.sum(jnp.where(opened, tiles_before + (tl - base_c), 0), axis=0, keepdims=True)
        tab_s[TAB_CLS:TAB_CLS + 1, :] = jnp.where(hit, t_cls, tab_s[TAB_CLS:TAB_CLS + 1, :])
        tab_s[TAB_ORD:TAB_ORD + 1, :] = jnp.where(hit, t_ord, tab_s[TAB_ORD:TAB_ORD + 1, :])
        used_new = used + jnp.sum(new_c, axis=0, keepdims=True)
        tab_s[TAB_USED:TAB_USED + 1, :] = jnp.broadcast_to(used_new, (1, TILE_LANES))
        cur_s[...] = jnp.broadcast_to(jnp.where(new_c > 0, base_c + new_c - 1, cur_c).astype(F32), cur_s.shape)
        cnt_s[...] = jnp.broadcast_to(cnt + n_c, cnt_s.shape)
        tab_ref[...] = tab_s[...]
        cnt_ref[...] = cnt_s[...]

        dest_v[...] = jnp.broadcast_to(dest, dest_v.shape)
        to_smem = pltpu.make_async_copy(dest_v.at[pl.ds(0, 1)], dest_sm.at[pl.ds(slot, 1)], dsem)
        to_smem.start()
        to_smem.wait()

        def wait_rows(s):
            pltpu.make_async_copy(xrow_s.at[s], xs_ref.at[pl.ds(0, tb)], rsem.at[s]).wait()

        wait_rows(prev)

        @pl.when(step == pl.num_programs(0) * nblk - 1)
        def _flush():
            _issue_rows(tb, xrow_s, slot, xs_ref, dest_sm, rsem)
            wait_rows(slot)

    kt = kbuf[tb:tb + WINDOW, :]
    vt = vbuf[tb:tb + WINDOW, :]
    kbuf[0:WINDOW, :] = kt
    vbuf[0:WINDOW, :] = vt
    kwin_ref[...] = kt
    vwin_ref[...] = vt
    for hh in range(B_HEADS):
        sout_ref[hh] = st_s[hh].T


def _layer_call(x, cache, weights, pos0, tb, n_sorted_tiles=None):
    bsz, seq, _ = x.shape
    nblk = seq // tb
    sorted_out = cache is None
    g1, w_in, sink, lower_bounds, hg, w_o, g2, w_r, b_r = weights

    def const(shape):
        return pl.BlockSpec(shape, lambda b, j: (0,) * len(shape))

    def per_stream(shape):
        return pl.BlockSpec((None,) + shape, lambda b, j: (b,) + (0,) * len(shape))

    in_specs = [pl.BlockSpec((None, tb, D_MODEL), lambda b, j: (b, j, 0))]
    args = [x]
    if not sorted_out:
        in_specs += [per_stream((WINDOW, A_KV_WIDTH)), per_stream((WINDOW, A_KV_WIDTH)),
                     per_stream((B_HEADS, B_KEY_DIM, B_VAL_DIM))]
        args += list(cache)
    in_specs += [const((1, D_MODEL)), const((D_MODEL, N_IN)),
                 pl.BlockSpec(memory_space=pltpu.SMEM),
                 const((2, B_KEY_WIDTH)), const((1, B_WIDTH)), const((MIX_WIDTH, D_MODEL)),
                 const((1, D_MODEL)), const((D_MODEL, LANES)), const((1, LANES))]
    args += [g1, w_in, sink, lower_bounds, hg, w_o, g2, w_r, b_r]
    stream_shapes = (jax.ShapeDtypeStruct((bsz, WINDOW, A_KV_WIDTH), F32),
                     jax.ShapeDtypeStruct((bsz, WINDOW, A_KV_WIDTH), F32),
                     jax.ShapeDtypeStruct((bsz, B_HEADS, B_KEY_DIM, B_VAL_DIM), F32))
    stream_specs = (per_stream((WINDOW, A_KV_WIDTH)), per_stream((WINDOW, A_KV_WIDTH)),
                    per_stream((B_HEADS, B_KEY_DIM, B_VAL_DIM)))
    scratch = [pltpu.VMEM((tb, N_IN), F32),
               pltpu.VMEM((WINDOW + tb, A_KV_WIDTH), F32),
               pltpu.VMEM((WINDOW + tb, A_KV_WIDTH), F32),
               pltpu.VMEM((B_HEADS, B_VAL_DIM, B_KEY_DIM), F32),
               pltpu.VMEM((tb, MIX_WIDTH), BF16),
               pltpu.VMEM((2 * A_KV_HEADS, A_GROUP * CHUNK, WINDOW + CHUNK), F32),
               pltpu.VMEM((2, CHUNK, B_KEY_WIDTH), F32),
               pltpu.VMEM((2, CHUNK, B_KEY_WIDTH), F32),
               pltpu.VMEM((2, CHUNK, B_KEY_WIDTH), F32),
               pltpu.VMEM((2, CHUNK, B_KEY_WIDTH), BF16),
               pltpu.VMEM((2, 1, B_KEY_WIDTH), F32),
               pltpu.VMEM((2 * B_HEADS, CHUNK, CHUNK), BF16),
               pltpu.VMEM((2 * B_HEADS, B_VAL_DIM, B_KEY_DIM), F32)]
    if sorted_out:
        out_shape = (jax.ShapeDtypeStruct((n_sorted_tiles * MOE_TILE, ROW_WIDTH), F32),
                     jax.ShapeDtypeStruct((SUBLANES, TILE_LANES), I32),
                     jax.ShapeDtypeStruct((CLS_ROWS, LANES), F32)) + stream_shapes
        out_specs = (pl.BlockSpec(memory_space=pl.ANY), const((SUBLANES, TILE_LANES)),
                     const((CLS_ROWS, LANES))) + stream_specs
        scratch += [pltpu.VMEM((2, tb, ROW_WIDTH), F32),
                    pltpu.VMEM((CLS_ROWS, LANES), F32),
                    pltpu.VMEM((CLS_ROWS, LANES), F32),
                    pltpu.VMEM((SUBLANES, TILE_LANES), I32),
                    pltpu.VMEM((SUBLANES, tb), I32),
                    pltpu.SMEM((2, tb), I32),
                    pltpu.SemaphoreType.DMA(()),
                    pltpu.SemaphoreType.DMA((2,))]
    else:
        out_shape = (jax.ShapeDtypeStruct((bsz, seq, D_MODEL), F32),
                     jax.ShapeDtypeStruct((bsz, seq, LANES), F32)) + stream_shapes
        out_specs = (pl.BlockSpec((None, tb, D_MODEL), lambda b, j: (b, j, 0)),
                     pl.BlockSpec((None, tb, LANES), lambda b, j: (b, j, 0))) + stream_specs
    return pl.pallas_call(
        functools.partial(_layer_body, pos0, tb, sorted_out),
        grid=(bsz, nblk),
        in_specs=in_specs,
        out_specs=out_specs,
        out_shape=out_shape,
        scratch_shapes=scratch,
        compiler_params=pltpu.CompilerParams(
            dimension_semantics=("arbitrary", "arbitrary"),
            vmem_limit_bytes=VMEM_LIMIT_BYTES),
        name="layer_prompt" if sorted_out else "layer_sample",
    )(*args)


def _moe_sorted_body(blk_ref, ea_ref, eb_ref, nv_ref, nused_ref, xs_ref, g2_ref,
                     wga_ref, wua_ref, wda_ref, wgb_ref, wub_ref, wdb_ref, gf_ref, y_ref,
                     ybuf, h2_s, ya_s, tok_v, tok_sm, tsem, rsem):
    p = pl.program_id(0)
    n_used = nused_ref[0]
    slot = p % 2
    prev = 1 - slot
    nv_prev = nv_ref[jnp.maximum(p - 1, 0)]
    nv_prev2 = nv_ref[jnp.maximum(p - 2, 0)]

    def expert(h2, wg_ref, wu_ref, wd_ref):
        a = jnp.dot(h2, wg_ref[...], preferred_element_type=F32)
        u = jnp.dot(h2, wu_ref[...], preferred_element_type=F32)
        return jnp.dot((_silu(a) * u).astype(BF16), wd_ref[...], preferred_element_type=F32)

    def tok_copy():
        return pltpu.make_async_copy(tok_v.at[pl.ds(0, 1)], tok_sm.at[pl.ds(slot, 1)], tsem)

    def first_half():
        meta = xs_ref[:, D_MODEL:ROW_WIDTH]
        sr = lax.broadcasted_iota(I32, (SUBLANES, LANES), 0)
        sc = lax.broadcasted_iota(I32, (SUBLANES, LANES), 1)
        pick = jnp.where(sc == sr + META_TOK_HI, 1.0, 0.0).astype(BF16)
        ids = lax.dot_general(pick, meta.astype(BF16), NT_DIMS, preferred_element_type=F32)
        tok = (ids[0:1, :] * TOK_SPLIT + ids[1:2, :]).astype(I32)
        tok_v[...] = jnp.broadcast_to(tok, tok_v.shape)
        tok_copy().start()
        h2 = _rmsnorm(xs_ref[:, 0:D_MODEL], g2_ref[...]).astype(BF16)
        h2_s[...] = h2
        ya_s[...] = expert(h2, wga_ref, wua_ref, wda_ref)

    def second_half():
        x1 = xs_ref[:, 0:D_MODEL]
        meta = xs_ref[:, D_MODEL:ROW_WIDTH]
        yb = expert(h2_s[...], wgb_ref, wub_ref, wdb_ref)
        moe = meta[:, META_GA:META_GA + 1] * ya_s[...] + meta[:, META_GB:META_GB + 1] * yb
        ybuf[slot] = _rmsnorm(x1 + moe, gf_ref[...])
        tok_copy().wait()

    def send_full():
        _issue_rows(MOE_TILE // 2, ybuf, prev, y_ref, tok_sm, rsem)

    def send_full_rest():
        _issue_rows(MOE_TILE, ybuf, prev, y_ref, tok_sm, rsem, first=MOE_TILE // 2)

    def send_part():
        def start(r, carry):
            _row_copy(ybuf.at[prev], r, y_ref, tok_sm[prev, r], rsem.at[prev]).start()
            return carry

        lax.fori_loop(0, nv_prev, start, 0)

    def wait_full():
        pltpu.make_async_copy(ybuf.at[slot], y_ref.at[pl.ds(0, MOE_TILE)], rsem.at[slot]).wait()

    def wait_part():
        def wait(r, carry):
            _row_copy(ybuf.at[slot], r, y_ref, 0, rsem.at[slot]).wait()
            return carry

        lax.fori_loop(0, nv_prev2, wait, 0)

    has_prev = (p >= 1) & (p <= n_used)
    prev_full = nv_prev == MOE_TILE
    prev_part = jnp.logical_not(prev_full)
    has_prev2 = (p >= 2) & (p <= n_used + 1)
    prev2_full = nv_prev2 == MOE_TILE
    active = p < n_used
    idle = jnp.logical_not(active)
    steady = active & has_prev & prev_full

    @pl.when(steady)
    def _steady_first():
        send_full()
        first_half()

    @pl.when(active & has_prev & prev_part)
    def _after_partial():
        send_part()
        first_half()

    @pl.when(active & jnp.logical_not(has_prev))
    def _first():
        first_half()

    @pl.when(idle & has_prev & prev_full)
    def _flush_full():
        send_full()
        send_full_rest()

    @pl.when(idle & has_prev & prev_part)
    def _flush_part():
        send_part()

    @pl.when(has_prev2 & prev2_full)
    def _wait_full():
        wait_full()

    @pl.when(has_prev2 & jnp.logical_not(prev2_full))
    def _wait_part():
        wait_part()

    @pl.when(steady)
    def _steady_second():
        send_full_rest()
        second_half()

    @pl.when(active & jnp.logical_not(steady))
    def _second():
        second_half()


def _moe_sorted_call(step_blk, step_ea, step_eb, step_nv, n_used, xs, g2, wg, wu, wd, gf, n_tok):
    n_steps = step_blk.shape[0]

    def const(shape):
        return pl.BlockSpec(shape, lambda p, blk, ea, eb, nv, nu: (0,) * len(shape))

    def w_a(shape):
        return pl.BlockSpec((None,) + shape, lambda p, blk, ea, eb, nv, nu: (ea[p], 0, 0))

    def w_b(shape):
        return pl.BlockSpec((None,) + shape, lambda p, blk, ea, eb, nv, nu: (eb[p], 0, 0))

    return pl.pallas_call(
        _moe_sorted_body,
        grid_spec=pltpu.PrefetchScalarGridSpec(
            num_scalar_prefetch=5,
            grid=(n_steps,),
            in_specs=[pl.BlockSpec((MOE_TILE, ROW_WIDTH), lambda p, blk, ea, eb, nv, nu: (blk[p], 0)),
                      const((1, D_MODEL)),
                      w_a((D_MODEL, D_EXPERT)), w_a((D_MODEL, D_EXPERT)), w_a((D_EXPERT, D_MODEL)),
                      w_b((D_MODEL, D_EXPERT)), w_b((D_MODEL, D_EXPERT)), w_b((D_EXPERT, D_MODEL)),
                      const((1, D_MODEL))],
            out_specs=pl.BlockSpec(memory_space=pl.ANY),
            scratch_shapes=[pltpu.VMEM((2, MOE_TILE, D_MODEL), F32),
                            pltpu.VMEM((MOE_TILE, D_MODEL), BF16),
                            pltpu.VMEM((MOE_TILE, D_MODEL), F32),
                            pltpu.VMEM((SUBLANES, MOE_TILE), I32),
                            pltpu.SMEM((2, MOE_TILE), I32),
                            pltpu.SemaphoreType.DMA(()),
                            pltpu.SemaphoreType.DMA((2,))]),
        out_shape=jax.ShapeDtypeStruct((n_tok, D_MODEL), F32),
        compiler_params=pltpu.CompilerParams(
            dimension_semantics=("arbitrary",), vmem_limit_bytes=VMEM_LIMIT_BYTES),
        name="moe_sorted",
    )(step_blk, step_ea, step_eb, step_nv, n_used, xs, g2, wg, wu, wd, wg, wu, wd, gf)


def _moe_dense_body(x1_ref, route_ref, g2_ref, wg_ref, wu_ref, wd_ref, gf_ref, y_ref, h2_s, acc_s):
    e = pl.program_id(1)

    @pl.when(e == 0)
    def _first():
        h2_s[...] = _rmsnorm(x1_ref[...], g2_ref[...]).astype(BF16)
        acc_s[...] = jnp.zeros(acc_s.shape, F32)

    h2 = h2_s[...]
    a = jnp.dot(h2, wg_ref[...], preferred_element_type=F32)
    u = jnp.dot(h2, wu_ref[...], preferred_element_type=F32)
    y = jnp.dot((_silu(a) * u).astype(BF16), wd_ref[...], preferred_element_type=F32)
    route = route_ref[...]
    lane = lax.broadcasted_iota(I32, route.shape, 1)
    gate = jnp.sum(jnp.where(lane == e + ROUTE_E0, route, 0.0), axis=-1, keepdims=True)
    acc_s[...] += gate * y

    @pl.when(e == N_EXPERTS - 1)
    def _last():
        y_ref[...] = _rmsnorm(x1_ref[...] + acc_s[...], gf_ref[...])


def _moe_dense_call(x1, route, g2, wg, wu, wd, gf, tm):
    n = x1.shape[0]
    return pl.pallas_call(
        _moe_dense_body,
        grid=(n // tm, N_EXPERTS),
        in_specs=[pl.BlockSpec((tm, D_MODEL), lambda i, e: (i, 0)),
                  pl.BlockSpec((tm, LANES), lambda i, e: (i, 0)),
                  pl.BlockSpec((1, D_MODEL), lambda i, e: (0, 0)),
                  pl.BlockSpec((None, D_MODEL, D_EXPERT), lambda i, e: (e, 0, 0)),
                  pl.BlockSpec((None, D_MODEL, D_EXPERT), lambda i, e: (e, 0, 0)),
                  pl.BlockSpec((None, D_EXPERT, D_MODEL), lambda i, e: (e, 0, 0)),
                  pl.BlockSpec((1, D_MODEL), lambda i, e: (0, 0))],
        out_specs=pl.BlockSpec((tm, D_MODEL), lambda i, e: (i, 0)),
        out_shape=jax.ShapeDtypeStruct((n, D_MODEL), F32),
        scratch_shapes=[pltpu.VMEM((tm, D_MODEL), BF16), pltpu.VMEM((tm, D_MODEL), F32)],
        compiler_params=pltpu.CompilerParams(
            dimension_semantics=("arbitrary", "arbitrary"),
            vmem_limit_bytes=VMEM_LIMIT_BYTES),
        name="moe_dense",
    )(x1, route, g2, wg, wu, wd, gf)


def _pick_block(seq, target):
    tb = min(seq, target)
    assert seq % tb == 0 and tb % CHUNK == 0
    return tb


def _lookup(table, idx):
    table = jnp.asarray(table, I32)
    k = jnp.arange(table.shape[0], dtype=I32)
    return jnp.sum(jnp.where(idx[..., None] == k, table, 0), axis=-1).astype(I32)


def _plan_steps(tab, counts, n_tiles):
    tile = jnp.arange(n_tiles, dtype=I32)
    n_used = tab[TAB_USED, 0]
    t_cls = tab[TAB_CLS, :n_tiles]
    t_ord = tab[TAB_ORD, :n_tiles]
    live = tile < n_used
    key = jnp.where(live, t_cls * n_tiles + tile, N_CLASSES * n_tiles + tile)
    pos = jnp.sum(key[None, :] < key[:, None], axis=1).astype(I32)
    step = jnp.arange(n_tiles + 2, dtype=I32)
    step_c = jnp.minimum(step, n_used - 1)
    step_tile = jnp.sum(jnp.where(pos[None, :] == step_c[:, None], tile[None, :], 0), axis=1).astype(I32)
    s_cls = _lookup(t_cls, step_tile)
    s_ord = _lookup(t_ord, step_tile)
    s_nv = jnp.clip(_lookup(counts, s_cls) - s_ord * MOE_TILE, 0, MOE_TILE).astype(I32)
    cls_ea = np.array([g * EXPERTS_PER_GROUP + a for g in range(N_GROUPS) for a, _ in PAIRS], np.int32)
    cls_eb = np.array([g * EXPERTS_PER_GROUP + b for g in range(N_GROUPS) for _, b in PAIRS], np.int32)
    return step_tile, _lookup(cls_ea, s_cls), _lookup(cls_eb, s_cls), s_nv, n_used.reshape(1).astype(I32)


def kernel(x_prompt, x_sample, cache_k, cache_v, state_hgrn, norm1_g, w_in, attn_sink, lower_bounds,
           hgrn_norm_g, w_o, norm2_g, w_router_group, b_router_group, w_router_expert, b_router_expert,
           w_gate, w_up, w_down, final_norm_g):
    depth = w_in.shape[0]
    assert depth == 1
    l = 0
    w_hist = cache_k.shape[2]
    assert w_hist == WINDOW
    w_r = jnp.concatenate(
        [w_router_group[l], jnp.transpose(w_router_expert[l], (1, 0, 2)).reshape(D_MODEL, N_EXPERTS)], axis=1)
    w_r = jnp.pad(w_r, ((0, 0), (0, LANES - w_r.shape[1]))).astype(BF16)
    b_r = jnp.concatenate([b_router_group[l], b_router_expert[l].reshape(N_EXPERTS)])
    b_r = jnp.pad(b_r, (0, LANES - b_r.shape[0])).reshape(1, LANES).astype(F32)
    weights = (norm1_g[l].reshape(1, D_MODEL), w_in[l].astype(BF16), attn_sink[l].astype(F32),
               lower_bounds.astype(F32), hgrn_norm_g[l].reshape(1, B_WIDTH), w_o[l].astype(BF16),
               norm2_g[l].reshape(1, D_MODEL), w_r, b_r)
    g2 = norm2_g[l].reshape(1, D_MODEL)
    gf = final_norm_g.reshape(1, D_MODEL)
    wg, wu, wd = w_gate[l].astype(BF16), w_up[l].astype(BF16), w_down[l].astype(BF16)

    bp, lp, _ = x_prompt.shape
    bs, ls, _ = x_sample.shape
    np_tok, ns_tok = bp * lp, bs * ls
    tbp, tbs = _pick_block(lp, LAYER_BLOCK), _pick_block(ls, LAYER_BLOCK)
    assert tbp % MOE_TILE == 0 and np_tok % MOE_TILE == 0 and np_tok < TOK_SPLIT * TOK_SPLIT

    n_tiles = np_tok // MOE_TILE + N_CLASSES
    assert n_tiles <= TILE_LANES
    xs, tab, cnt, kp, vp, sp = _layer_call(x_prompt, None, weights, 0, tbp, n_tiles + tbp // MOE_TILE)
    counts = cnt[:N_CLASSES, 0].astype(I32)
    step_blk, step_ea, step_eb, step_nv, n_used = _plan_steps(tab, counts, n_tiles)
    yp = _moe_sorted_call(step_blk, step_ea, step_eb, step_nv, n_used, xs, g2, wg, wu, wd, gf, np_tok)

    cache = (cache_k[l].reshape(bs, w_hist, A_KV_WIDTH), cache_v[l].reshape(bs, w_hist, A_KV_WIDTH), state_hgrn[l])
    x1s, routes, kn, vn, sn = _layer_call(x_sample, cache, weights, PAST_LEN, tbs)
    ys = _moe_dense_call(x1s.reshape(ns_tok, D_MODEL), routes.reshape(ns_tok, LANES), g2, wg, wu, wd, gf,
                         _pick_block(ns_tok, LAYER_BLOCK))

    kv_shape = (1, -1, w_hist, A_KV_HEADS, A_HEAD_DIM)
    return (yp.reshape(bp, lp, D_MODEL), ys.reshape(bs, ls, D_MODEL),
            kp.reshape(kv_shape), vp.reshape(kv_shape), sp[None],
            kn.reshape(kv_shape), vn.reshape(kv_shape), sn[None])
```

```python
import functools

import numpy as np
import jax
import jax.numpy as jnp
from jax import lax
from jax.experimental import pallas as pl
from jax.experimental.pallas import tpu as pltpu

F32 = jnp.float32
BF16 = jnp.bfloat16
I32 = jnp.int32

D_MODEL = 1024
CHUNK = 64
EPS = 1e-6
PAST_LEN = 4096
WINDOW = 128
A_HEADS = 8
A_KV_HEADS = 2
A_HEAD_DIM = 64
A_GROUP = A_HEADS // A_KV_HEADS
A_WIDTH = A_HEADS * A_HEAD_DIM
A_KV_WIDTH = A_KV_HEADS * A_HEAD_DIM
B_HEADS = 4
B_KEY_DIM = 128
B_VAL_DIM = 128
B_KEY_WIDTH = B_HEADS * B_KEY_DIM
B_WIDTH = B_HEADS * B_VAL_DIM
MIX_WIDTH = A_WIDTH + B_WIDTH
OFF_K = A_WIDTH
OFF_V = OFF_K + A_KV_WIDTH
OFF_QB = OFF_V + A_KV_WIDTH
OFF_FB = OFF_QB + B_KEY_WIDTH
OFF_IB = OFF_FB + B_KEY_WIDTH
OFF_GB = OFF_IB + B_WIDTH
N_IN = OFF_GB + B_WIDTH
N_GROUPS = 4
EXPERTS_PER_GROUP = 4
N_EXPERTS = N_GROUPS * EXPERTS_PER_GROUP
D_EXPERT = 256
LANES = 128
SUBLANES = 8
ROUTE_E0 = N_GROUPS
PAIRS = [(a, b) for a in range(EXPERTS_PER_GROUP) for b in range(a + 1, EXPERTS_PER_GROUP)]
N_PAIRS = len(PAIRS)
N_CLASSES = N_GROUPS * N_PAIRS
CLS_ROWS = 32
ROW_WIDTH = D_MODEL + LANES
META_GA, META_GB, META_TOK_HI, META_TOK_LO = 0, 1, 2, 3
TOK_SPLIT = 256
LAYER_BLOCK = 512
MOE_TILE = 256
MOE_TILE_SHIFT = 8
TILE_LANES = 256
TAB_CLS, TAB_ORD, TAB_USED = 0, 1, 2

VMEM_LIMIT_BYTES = 56 * 1024 * 1024

NT_DIMS = (((1,), (1,)), ((), ()))
TN_DIMS = (((0,), (0,)), ((), ()))


def _rmsnorm(x, g):
    return x * lax.rsqrt(jnp.mean(x * x, axis=-1, keepdims=True) + EPS) * g


def _silu(x):
    hx = 0.5 * x
    return hx + hx * jnp.tanh(hx)


def _route(lt):
    t = lt.shape[1]
    neg = -jnp.inf
    assert N_GROUPS == EXPERTS_PER_GROUP
    r4 = lax.broadcasted_iota(I32, (N_GROUPS, t), 0).astype(F32)

    def first_row(mask):
        return jnp.min(jnp.where(mask, r4, float(N_GROUPS)), axis=0, keepdims=True)

    gl = lt[0:N_GROUPS]
    gmax = jnp.max(gl, axis=0, keepdims=True)
    gidx = first_row(gl == gmax)
    p_group = 1.0 / jnp.sum(jnp.exp(gl - gmax), axis=0, keepdims=True)
    el = lt[ROUTE_E0:ROUTE_E0 + EXPERTS_PER_GROUP]
    for g in range(1, N_GROUPS):
        lo = ROUTE_E0 + g * EXPERTS_PER_GROUP
        el = jnp.where(gidx == g, lt[lo:lo + EXPERTS_PER_GROUP], el)
    e1 = jnp.max(el, axis=0, keepdims=True)
    i1 = first_row(el == e1)
    el2 = jnp.where(r4 == i1, neg, el)
    e2 = jnp.max(el2, axis=0, keepdims=True)
    i2 = first_row(el2 == e2)
    tt = jnp.exp(e2 - e1)
    w1 = p_group / (1.0 + tt)
    w2 = w1 * tt
    first_low = i1 < i2
    ea = jnp.where(first_low, i1, i2)
    eb = jnp.where(first_low, i2, i1)
    ga = jnp.where(first_low, w1, w2)
    gb = jnp.where(first_low, w2, w1)
    pair = 0.5 * (ea * (2 * EXPERTS_PER_GROUP - 1 - ea)) + eb - ea - 1.0
    rows = lax.broadcasted_iota(I32, (CLS_ROWS, t), 0).astype(F32)
    e_row0 = ROUTE_E0 + EXPERTS_PER_GROUP * gidx
    dense = jnp.where(rows == e_row0 + i1, w1, 0.0) + jnp.where(rows == e_row0 + i2, w2, 0.0)
    return gidx * N_PAIRS + pair, ga, gb, dense


def _rows_to_lanes(rows_t):
    r, t = rows_t.shape
    padded = jnp.concatenate([rows_t, jnp.zeros((LANES - r, t), F32)], axis=0)
    return padded.T


def _mixer_and_router(pos0, tb, j, x_ref, g1_ref, win_ref, sink_ref, lb_ref, hg_ref, wo_ref, g2_ref, wr_ref, br_ref,
                      proj_s, kbuf, vbuf, st_s, mix_s, sc_s, cum_s, kf_s, qf_s, qcum_s, dec_s, a_s, ds_s):
    nc = tb // CHUNK
    nk = WINDOW + CHUNK
    x = x_ref[...]
    h = _rmsnorm(x, g1_ref[...]).astype(BF16)
    proj_s[...] = jnp.dot(h, win_ref[...], preferred_element_type=F32)
    kbuf[WINDOW:WINDOW + tb, :] = proj_s[:, OFF_K:OFF_K + A_KV_WIDTH]
    vbuf[WINDOW:WINDOW + tb, :] = proj_s[:, OFF_V:OFF_V + A_KV_WIDTH]

    row = lax.broadcasted_iota(I32, (A_GROUP * CHUNK, nk), 0)
    col = lax.broadcasted_iota(I32, (A_GROUP * CHUNK, nk), 1)
    dist = jnp.abs((row & (CHUNK - 1)) - (col - WINDOW)).astype(F32)
    row_head = row // CHUNK
    rowc_head = lax.broadcasted_iota(I32, (A_GROUP * CHUNK, 1), 0) // CHUNK
    bias = []
    sinkc = []
    for hk in range(A_KV_HEADS):
        slope = jnp.exp2(-(row_head + (hk * A_GROUP + 1)).astype(F32))
        bias.append(slope * dist)
        sc = jnp.zeros((A_GROUP * CHUNK, 1), F32)
        for g in range(A_GROUP):
            sc = jnp.where(rowc_head == g, sink_ref[hk * A_GROUP + g], sc)
        sinkc.append(sc)

    lbr = lb_ref[...]
    lbm = jnp.max(lbr, axis=0, keepdims=True)
    lbe = jnp.exp(lbr - lbm)
    lb = lbe[0:1, :] / jnp.sum(lbe, axis=0, keepdims=True)
    f_mid = 0.5 * (1.0 + lb)
    f_half = 0.5 * (1.0 - lb)
    hg = hg_ref[...]
    tr = lax.broadcasted_iota(I32, (CHUNK, CHUNK), 0)
    tc = lax.broadcasted_iota(I32, (CHUNK, CHUNK), 1)
    tril = tr >= tc
    ltri = jnp.where(tril, 1.0, 0.0).astype(BF16)

    def chunk_rows(c):
        r0 = pl.multiple_of(c * CHUNK, CHUNK)
        return r0, pl.ds(r0, CHUNK)

    def stage_scores(c, par):
        r0, rows = chunk_rows(c)
        for hk in range(A_KV_HEADS):
            qc = proj_s[rows, hk * A_GROUP * A_HEAD_DIM:(hk + 1) * A_GROUP * A_HEAD_DIM] * (A_HEAD_DIM ** -0.5)
            q4 = jnp.concatenate([qc[:, g * A_HEAD_DIM:(g + 1) * A_HEAD_DIM] for g in range(A_GROUP)],
                                 axis=0).astype(BF16)
            kw = kbuf[pl.ds(r0, nk), hk * A_HEAD_DIM:(hk + 1) * A_HEAD_DIM].astype(BF16)
            sc_s[par * A_KV_HEADS + hk] = lax.dot_general(q4, kw, NT_DIMS, preferred_element_type=F32)
        qb = proj_s[rows, OFF_QB:OFF_QB + B_KEY_WIDTH]
        fl = proj_s[rows, OFF_FB:OFF_FB + B_KEY_WIDTH]
        f = f_mid + f_half * jnp.tanh(0.5 * fl)
        logf = jnp.log(f)
        hi = logf.astype(BF16)
        lo = (logf - hi.astype(F32)).astype(BF16)
        cum_s[par] = (jnp.dot(ltri, hi, preferred_element_type=F32)
                      + jnp.dot(ltri, lo, preferred_element_type=F32))
        kf_s[par] = 1.0 - f
        qf_s[par] = _silu(qb)

    def stage_softmax(c, par):
        r0, rows = chunk_rows(c)
        valid = col >= (WINDOW - pos0) - (j * tb + c * CHUNK)
        for hk in range(A_KV_HEADS):
            vw = vbuf[pl.ds(r0, nk), hk * A_HEAD_DIM:(hk + 1) * A_HEAD_DIM].astype(BF16)
            s = jnp.where(valid, sc_s[par * A_KV_HEADS + hk] - bias[hk], -jnp.inf)
            m = jnp.maximum(jnp.max(s, axis=-1, keepdims=True), sinkc[hk])
            p = jnp.exp(s - m)
            den = jnp.sum(p, axis=-1, keepdims=True) + jnp.exp(sinkc[hk] - m)
            o = jnp.dot(p.astype(BF16), vw, preferred_element_type=F32) / den
            att = jnp.concatenate([o[g * CHUNK:(g + 1) * CHUNK] for g in range(A_GROUP)], axis=1)
            mix_s[rows, hk * A_GROUP * A_HEAD_DIM:(hk + 1) * A_GROUP * A_HEAD_DIM] = att.astype(BF16)
        cum = cum_s[par]
        qf = qf_s[par]
        kf = kf_s[par]
        vb = proj_s[rows, OFF_IB:OFF_IB + B_WIDTH].astype(BF16)
        ref = cum[CHUNK // 2:CHUNK // 2 + 1, :]
        tot = cum[CHUNK - 1:CHUNK, :]
        e_fwd = jnp.exp(cum - ref)
        e_bwd = 1.0 / e_fwd
        q_rel = qf * e_fwd
        k_rel = kf * e_bwd
        q_in = q_rel.astype(BF16)
        k_in = k_rel.astype(BF16)
        k_tot = (k_rel * jnp.exp(tot - ref)).astype(BF16)
        qcum_s[par] = (q_rel * jnp.exp(ref)).astype(BF16)
        dec_s[par] = jnp.exp(tot)
        for hh in range(B_HEADS):
            sl = slice(hh * B_KEY_DIM, (hh + 1) * B_KEY_DIM)
            a = lax.dot_general(q_in[:, sl], k_in[:, sl], NT_DIMS, preferred_element_type=F32)
            a_s[par * B_HEADS + hh] = jnp.where(tril, a, 0.0).astype(BF16)
            ds_s[par * B_HEADS + hh] = lax.dot_general(vb[:, sl], k_tot[:, sl], TN_DIMS,
                                                       preferred_element_type=F32)

    def stage_state(c, par):
        r0, rows = chunk_rows(c)
        vb = proj_s[rows, OFF_IB:OFF_IB + B_WIDTH].astype(BF16)
        gg = proj_s[rows, OFF_GB:OFF_GB + B_WIDTH]
        dec = dec_s[par]
        outs = []
        for hh in range(B_HEADS):
            sl = slice(hh * B_KEY_DIM, (hh + 1) * B_KEY_DIM)
            st = st_s[hh]
            o = (jnp.dot(a_s[par * B_HEADS + hh], vb[:, sl], preferred_element_type=F32)
                 + lax.dot_general(qcum_s[par, :, sl], st.astype(BF16), NT_DIMS, preferred_element_type=F32))
            st_s[hh] = st * dec[:, sl] + ds_s[par * B_HEADS + hh]
            o = o * lax.rsqrt(jnp.mean(o * o, axis=-1, keepdims=True) + EPS) * hg[:, sl]
            outs.append(o)
        rec = jnp.concatenate(outs, axis=1) * _silu(gg)
        mix_s[rows, A_WIDTH:A_WIDTH + B_WIDTH] = rec.astype(BF16)

    if nc >= 4 and nc % 2 == 0:
        stage_scores(0, 0)
        stage_scores(1, 1)
        stage_softmax(0, 0)

        def pipelined(k, carry):
            for u in range(2):
                i = 2 * k + u
                stage_scores(i + 2, u)
                stage_softmax(i + 1, 1 - u)
                stage_state(i, u)
            return carry

        lax.fori_loop(0, (nc - 2) // 2, pipelined, 0)
        stage_softmax(nc - 1, 1)
        stage_state(nc - 2, 0)
        stage_state(nc - 1, 1)
    else:
        for c in range(nc):
            stage_scores(c, c % 2)
            stage_softmax(c, c % 2)
            stage_state(c, c % 2)

    x1 = x + jnp.dot(mix_s[...], wo_ref[...], preferred_element_type=F32)
    h2 = _rmsnorm(x1, g2_ref[...]).astype(BF16)
    logits_t = lax.dot_general(wr_ref[...], h2, NT_DIMS, preferred_element_type=F32) + br_ref[...]
    return x1, logits_t


def _row_copy(src_ref, src_row, dst_ref, dst_row, sem):
    return pltpu.make_async_copy(src_ref.at[pl.ds(src_row, 1)], dst_ref.at[pl.ds(dst_row, 1)], sem)


def _issue_rows(n, src_ref, src_slot, dst_ref, idx_ref, sem, first=0):
    for r in range(first, n):
        _row_copy(src_ref.at[src_slot], r, dst_ref, idx_ref[src_slot, r], sem.at[src_slot]).start(priority=r % 2)


def _layer_body(pos0, tb, sorted_out, *refs):
    if sorted_out:
        (x_ref, g1_ref, win_ref, sink_ref, lb_ref, hg_ref, wo_ref, g2_ref, wr_ref, br_ref,
         xs_ref, tab_ref, cnt_ref, kwin_ref, vwin_ref, sout_ref,
         proj_s, kbuf, vbuf, st_s, mix_s, *pipe_s, xrow_s, cnt_s, cur_s, tab_s, dest_v, dest_sm, dsem, rsem) = refs
    else:
        (x_ref, ck_ref, cv_ref, s0_ref, g1_ref, win_ref, sink_ref, lb_ref, hg_ref, wo_ref, g2_ref, wr_ref, br_ref,
         x1_ref, route_ref, kwin_ref, vwin_ref, sout_ref,
         proj_s, kbuf, vbuf, st_s, mix_s, *pipe_s) = refs
    b = pl.program_id(0)
    j = pl.program_id(1)
    nblk = pl.num_programs(1)
    step = b * nblk + j

    if sorted_out:
        slot = step % 2
        prev = 1 - slot
        n_tiles = xs_ref.shape[0] // MOE_TILE
        spare_row0 = (n_tiles - tb // MOE_TILE) * MOE_TILE

        @pl.when(step == 0)
        def _init_routing():
            cnt_s[...] = jnp.zeros(cnt_s.shape, F32)
            cur_s[...] = jnp.zeros(cur_s.shape, F32)
            tab_s[...] = jnp.zeros(tab_s.shape, I32)
            xrow_s[1] = jnp.zeros(xrow_s.shape[1:], F32)

            def fill(r, carry):
                dest_sm[1, r] = spare_row0 + r
                return carry

            lax.fori_loop(0, tb, fill, 0)

        def dest_copy(s):
            return pltpu.make_async_copy(dest_v.at[pl.ds(0, 1)], dest_sm.at[pl.ds(s, 1)], dsem)

        @pl.when(step > 0)
        def _dest_landed():
            dest_copy(prev).wait()

    @pl.when(j == 0)
    def _init_stream():
        if sorted_out:
            kbuf[0:WINDOW, :] = jnp.zeros((WINDOW, A_KV_WIDTH), F32)
            vbuf[0:WINDOW, :] = jnp.zeros((WINDOW, A_KV_WIDTH), F32)
            st_s[...] = jnp.zeros(st_s.shape, F32)
        else:
            kbuf[0:WINDOW, :] = ck_ref[...]
            vbuf[0:WINDOW, :] = cv_ref[...]
            for hh in range(B_HEADS):
                st_s[hh] = s0_ref[hh].T

    if sorted_out:
        _issue_rows(tb, xrow_s, prev, xs_ref, dest_sm, rsem)

    x1, logits_t = _mixer_and_router(pos0, tb, j, x_ref, g1_ref, win_ref, sink_ref, lb_ref, hg_ref, wo_ref, g2_ref,
                                   wr_ref, br_ref, proj_s, kbuf, vbuf, st_s, mix_s, *pipe_s)
    cls, ga, gb, dense = _route(logits_t)

    if not sorted_out:
        x1_ref[...] = x1
        route_ref[...] = _rows_to_lanes(dense)
    else:
        tok = step * tb + lax.broadcasted_iota(I32, (1, tb), 1)
        tok_hi = (tok >> 8).astype(F32)
        tok_lo = (tok & (TOK_SPLIT - 1)).astype(F32)
        assert (META_GA, META_GB, META_TOK_HI, META_TOK_LO) == (0, 1, 2, 3)
        xrow_s[slot, :, 0:D_MODEL] = x1
        xrow_s[slot, :, D_MODEL:ROW_WIDTH] = _rows_to_lanes(jnp.concatenate([ga, gb, tok_hi, tok_lo], axis=0))

        oht = jnp.where(lax.broadcasted_iota(I32, (CLS_ROWS, tb), 0).astype(F32) == cls, 1.0, 0.0)
        ur = lax.broadcasted_iota(I32, (tb, tb), 0)
        uc = lax.broadcasted_iota(I32, (tb, tb), 1)
        before = jnp.where(ur < uc, 1.0, 0.0).astype(BF16)
        prefix = jnp.dot(oht.astype(BF16), before, preferred_element_type=F32)
        cnt = cnt_s[:, 0:1]
        rank = jnp.sum(oht * (prefix + cnt), axis=0, keepdims=True).astype(I32)

        n_c = jnp.sum(oht, axis=1, keepdims=True)
        cnt_i = cnt.astype(I32)
        after_i = (cnt + n_c).astype(I32)
        tiles_before = (cnt_i + (MOE_TILE - 1)) >> MOE_TILE_SHIFT
        new_c = ((after_i + (MOE_TILE - 1)) >> MOE_TILE_SHIFT) - tiles_before
        sr = lax.broadcasted_iota(I32, (CLS_ROWS, CLS_ROWS), 0)
        sc = lax.broadcasted_iota(I32, (CLS_ROWS, CLS_ROWS), 1)
        lower = jnp.where(sr > sc, 1.0, 0.0).astype(BF16)
        new_b = jnp.broadcast_to(new_c.astype(F32), (CLS_ROWS, LANES)).astype(BF16)
        opened_before = jnp.dot(lower, new_b, preferred_element_type=F32)[:, 0:1].astype(I32)
        used = tab_s[TAB_USED:TAB_USED + 1, 0:1]
        base_c = used + opened_before
        partial = (cnt_i & (MOE_TILE - 1)) != 0
        cur_c = cur_s[:, 0:1].astype(I32)
        first_c = jnp.where(partial, cur_c, base_c)
        shift_c = jnp.where(partial, base_c - 1, base_c)
        ord0_c = cnt_i >> MOE_TILE_SHIFT

        def per_token(col):
            return jnp.sum(oht * col.astype(F32), axis=0, keepdims=True).astype(I32)

        ord_t = rank >> MOE_TILE_SHIFT
        d_t = ord_t - per_token(ord0_c)
        tile_t = jnp.where(d_t == 0, per_token(first_c), per_token(shift_c) + d_t)
        dest = tile_t * MOE_TILE + (rank & (MOE_TILE - 1))

        tl = lax.broadcasted_iota(I32, (CLS_ROWS, TILE_LANES), 1)
        opened = (tl >= base_c) & (tl < base_c + new_c)
        cls_id = lax.broadcasted_iota(I32, (CLS_ROWS, TILE_LANES), 0)
        hit = jnp.sum(jnp.where(opened, 1, 0), axis=0, keepdims=True) > 0
        t_cls = jnp.sum(jnp.where(opened, cls_id, 0), axis=0, keepdims=True)
        t_ord = jnp.sum(jnp.where(opened, tiles_before + (tl - base_c), 0), axis=0, keepdims=True)
        tab_s[TAB_CLS:TAB_CLS + 1, :] = jnp.where(hit, t_cls, tab_s[TAB_CLS:TAB_CLS + 1, :])
        tab_s[TAB_ORD:TAB_ORD + 1, :] = jnp.where(hit, t_ord, tab_s[TAB_ORD:TAB_ORD + 1, :])
        used_new = used + jnp.sum(new_c, axis=0, keepdims=True)
        tab_s[TAB_USED:TAB_USED + 1, :] = jnp.broadcast_to(used_new, (1, TILE_LANES))
        cur_s[...] = jnp.broadcast_to(jnp.where(new_c > 0, base_c + new_c - 1, cur_c).astype(F32), cur_s.shape)
        cnt_s[...] = jnp.broadcast_to(cnt + n_c, cnt_s.shape)
        tab_ref[...] = tab_s[...]
        cnt_ref[...] = cnt_s[...]

        dest_v[...] = jnp.broadcast_to(dest, dest_v.shape)
        dest_copy(slot).start()

        def wait_rows(s):
            pltpu.make_async_copy(xrow_s.at[s], xs_ref.at[pl.ds(0, tb)], rsem.at[s]).wait()

        wait_rows(prev)

        @pl.when(step == pl.num_programs(0) * nblk - 1)
        def _flush():
            dest_copy(slot).wait()
            _issue_rows(tb, xrow_s, slot, xs_ref, dest_sm, rsem)
            wait_rows(slot)

    kt = kbuf[tb:tb + WINDOW, :]
    vt = vbuf[tb:tb + WINDOW, :]
    kbuf[0:WINDOW, :] = kt
    vbuf[0:WINDOW, :] = vt
    kwin_ref[...] = kt
    vwin_ref[...] = vt
    for hh in range(B_HEADS):
        sout_ref[hh] = st_s[hh].T


def _layer_call(x, cache, weights, pos0, tb, n_sorted_tiles=None):
    bsz, seq, _ = x.shape
    nblk = seq // tb
    sorted_out = cache is None
    g1, w_in, sink, lower_bounds, hg, w_o, g2, w_r, b_r = weights

    def const(shape):
        return pl.BlockSpec(shape, lambda b, j: (0,) * len(shape))

    def per_stream(shape):
        return pl.BlockSpec((None,) + shape, lambda b, j: (b,) + (0,) * len(shape))

    in_specs = [pl.BlockSpec((None, tb, D_MODEL), lambda b, j: (b, j, 0))]
    args = [x]
    if not sorted_out:
        in_specs += [per_stream((WINDOW, A_KV_WIDTH)), per_stream((WINDOW, A_KV_WIDTH)),
                     per_stream((B_HEADS, B_KEY_DIM, B_VAL_DIM))]
        args += list(cache)
    in_specs += [const((1, D_MODEL)), const((D_MODEL, N_IN)),
                 pl.BlockSpec(memory_space=pltpu.SMEM),
                 const((2, B_KEY_WIDTH)), const((1, B_WIDTH)), const((MIX_WIDTH, D_MODEL)),
                 const((1, D_MODEL)), const((CLS_ROWS, D_MODEL)), const((CLS_ROWS, 1))]
    args += [g1, w_in, sink, lower_bounds, hg, w_o, g2, w_r, b_r]
    stream_shapes = (jax.ShapeDtypeStruct((bsz, WINDOW, A_KV_WIDTH), F32),
                     jax.ShapeDtypeStruct((bsz, WINDOW, A_KV_WIDTH), F32),
                     jax.ShapeDtypeStruct((bsz, B_HEADS, B_KEY_DIM, B_VAL_DIM), F32))
    stream_specs = (per_stream((WINDOW, A_KV_WIDTH)), per_stream((WINDOW, A_KV_WIDTH)),
                    per_stream((B_HEADS, B_KEY_DIM, B_VAL_DIM)))
    scratch = [pltpu.VMEM((tb, N_IN), F32),
               pltpu.VMEM((WINDOW + tb, A_KV_WIDTH), F32),
               pltpu.VMEM((WINDOW + tb, A_KV_WIDTH), F32),
               pltpu.VMEM((B_HEADS, B_VAL_DIM, B_KEY_DIM), F32),
               pltpu.VMEM((tb, MIX_WIDTH), BF16),
               pltpu.VMEM((2 * A_KV_HEADS, A_GROUP * CHUNK, WINDOW + CHUNK), F32),
               pltpu.VMEM((2, CHUNK, B_KEY_WIDTH), F32),
               pltpu.VMEM((2, CHUNK, B_KEY_WIDTH), F32),
               pltpu.VMEM((2, CHUNK, B_KEY_WIDTH), F32),
               pltpu.VMEM((2, CHUNK, B_KEY_WIDTH), BF16),
               pltpu.VMEM((2, 1, B_KEY_WIDTH), F32),
               pltpu.VMEM((2 * B_HEADS, CHUNK, CHUNK), BF16),
               pltpu.VMEM((2 * B_HEADS, B_VAL_DIM, B_KEY_DIM), F32)]
    if sorted_out:
        out_shape = (jax.ShapeDtypeStruct((n_sorted_tiles * MOE_TILE, ROW_WIDTH), F32),
                     jax.ShapeDtypeStruct((SUBLANES, TILE_LANES), I32),
                     jax.ShapeDtypeStruct((CLS_ROWS, LANES), F32)) + stream_shapes
        out_specs = (pl.BlockSpec(memory_space=pl.ANY), const((SUBLANES, TILE_LANES)),
                     const((CLS_ROWS, LANES))) + stream_specs
        scratch += [pltpu.VMEM((2, tb, ROW_WIDTH), F32),
                    pltpu.VMEM((CLS_ROWS, LANES), F32),
                    pltpu.VMEM((CLS_ROWS, LANES), F32),
                    pltpu.VMEM((SUBLANES, TILE_LANES), I32),
                    pltpu.VMEM((SUBLANES, tb), I32),
                    pltpu.SMEM((2, tb), I32),
                    pltpu.SemaphoreType.DMA(()),
                    pltpu.SemaphoreType.DMA((2,))]
    else:
        out_shape = (jax.ShapeDtypeStruct((bsz, seq, D_MODEL), F32),
                     jax.ShapeDtypeStruct((bsz, seq, LANES), F32)) + stream_shapes
        out_specs = (pl.BlockSpec((None, tb, D_MODEL), lambda b, j: (b, j, 0)),
                     pl.BlockSpec((None, tb, LANES), lambda b, j: (b, j, 0))) + stream_specs
    return pl.pallas_call(
        functools.partial(_layer_body, pos0, tb, sorted_out),
        grid=(bsz, nblk),
        in_specs=in_specs,
        out_specs=out_specs,
        out_shape=out_shape,
        scratch_shapes=scratch,
        compiler_params=pltpu.CompilerParams(
            dimension_semantics=("arbitrary", "arbitrary"),
            vmem_limit_bytes=VMEM_LIMIT_BYTES),
        name="layer_prompt" if sorted_out else "layer_sample",
    )(*args)


def _moe_sorted_body(blk_ref, ea_ref, eb_ref, nv_ref, nused_ref, xs_ref, g2_ref,
                     wga_ref, wua_ref, wda_ref, wgb_ref, wub_ref, wdb_ref, gf_ref, y_ref,
                     ybuf, h2_s, ya_s, tok_v, tok_sm, tsem, rsem):
    p = pl.program_id(0)
    n_used = nused_ref[0]
    slot = p % 2
    prev = 1 - slot
    nv_prev = nv_ref[jnp.maximum(p - 1, 0)]
    nv_prev2 = nv_ref[jnp.maximum(p - 2, 0)]

    def expert(h2, wg_ref, wu_ref, wd_ref):
        a = jnp.dot(h2, wg_ref[...], preferred_element_type=F32)
        u = jnp.dot(h2, wu_ref[...], preferred_element_type=F32)
        return jnp.dot((_silu(a) * u).astype(BF16), wd_ref[...], preferred_element_type=F32)

    def tok_copy():
        return pltpu.make_async_copy(tok_v.at[pl.ds(0, 1)], tok_sm.at[pl.ds(slot, 1)], tsem)

    def first_half():
        meta = xs_ref[:, D_MODEL:ROW_WIDTH]
        sr = lax.broadcasted_iota(I32, (SUBLANES, LANES), 0)
        sc = lax.broadcasted_iota(I32, (SUBLANES, LANES), 1)
        pick = jnp.where(sc == sr + META_TOK_HI, 1.0, 0.0).astype(BF16)
        ids = lax.dot_general(pick, meta.astype(BF16), NT_DIMS, preferred_element_type=F32)
        tok = (ids[0:1, :] * TOK_SPLIT + ids[1:2, :]).astype(I32)
        tok_v[...] = jnp.broadcast_to(tok, tok_v.shape)
        tok_copy().start()
        h2 = _rmsnorm(xs_ref[:, 0:D_MODEL], g2_ref[...]).astype(BF16)
        h2_s[...] = h2
        ya_s[...] = expert(h2, wga_ref, wua_ref, wda_ref)

    def second_half():
        x1 = xs_ref[:, 0:D_MODEL]
        meta = xs_ref[:, D_MODEL:ROW_WIDTH]
        yb = expert(h2_s[...], wgb_ref, wub_ref, wdb_ref)
        moe = meta[:, META_GA:META_GA + 1] * ya_s[...] + meta[:, META_GB:META_GB + 1] * yb
        ybuf[slot] = _rmsnorm(x1 + moe, gf_ref[...])
        tok_copy().wait()

    def send_full():
        _issue_rows(MOE_TILE // 2, ybuf, prev, y_ref, tok_sm, rsem)

    def send_full_rest():
        _issue_rows(MOE_TILE, ybuf, prev, y_ref, tok_sm, rsem, first=MOE_TILE // 2)

    def send_part():
        def start(r, carry):
            _row_copy(ybuf.at[prev], r, y_ref, tok_sm[prev, r], rsem.at[prev]).start()
            return carry

        lax.fori_loop(0, nv_prev, start, 0)

    def wait_full():
        pltpu.make_async_copy(ybuf.at[slot], y_ref.at[pl.ds(0, MOE_TILE)], rsem.at[slot]).wait()

    def wait_part():
        def wait(r, carry):
            _row_copy(ybuf.at[slot], r, y_ref, 0, rsem.at[slot]).wait()
            return carry

        lax.fori_loop(0, nv_prev2, wait, 0)

    has_prev = (p >= 1) & (p <= n_used)
    prev_full = nv_prev == MOE_TILE
    prev_part = jnp.logical_not(prev_full)
    has_prev2 = (p >= 2) & (p <= n_used + 1)
    prev2_full = nv_prev2 == MOE_TILE
    active = p < n_used
    idle = jnp.logical_not(active)
    steady = active & has_prev & prev_full

    @pl.when(steady)
    def _steady_first():
        send_full()
        first_half()

    @pl.when(active & has_prev & prev_part)
    def _after_partial():
        send_part()
        first_half()

    @pl.when(active & jnp.logical_not(has_prev))
    def _first():
        first_half()

    @pl.when(idle & has_prev & prev_full)
    def _flush_full():
        send_full()
        send_full_rest()

    @pl.when(idle & has_prev & prev_part)
    def _flush_part():
        send_part()

    @pl.when(has_prev2 & prev2_full)
    def _wait_full():
        wait_full()

    @pl.when(has_prev2 & jnp.logical_not(prev2_full))
    def _wait_part():
        wait_part()

    @pl.when(steady)
    def _steady_second():
        send_full_rest()
        second_half()

    @pl.when(active & jnp.logical_not(steady))
    def _second():
        second_half()


def _moe_sorted_call(step_blk, step_ea, step_eb, step_nv, n_used, xs, g2, wg, wu, wd, gf, n_tok):
    n_steps = step_blk.shape[0]

    def const(shape):
        return pl.BlockSpec(shape, lambda p, blk, ea, eb, nv, nu: (0,) * len(shape))

    def w_a(shape):
        return pl.BlockSpec((None,) + shape, lambda p, blk, ea, eb, nv, nu: (ea[p], 0, 0))

    def w_b(shape):
        return pl.BlockSpec((None,) + shape, lambda p, blk, ea, eb, nv, nu: (eb[p], 0, 0))

    return pl.pallas_call(
        _moe_sorted_body,
        grid_spec=pltpu.PrefetchScalarGridSpec(
            num_scalar_prefetch=5,
            grid=(n_steps,),
            in_specs=[pl.BlockSpec((MOE_TILE, ROW_WIDTH), lambda p, blk, ea, eb, nv, nu: (blk[p], 0)),
                      const((1, D_MODEL)),
                      w_a((D_MODEL, D_EXPERT)), w_a((D_MODEL, D_EXPERT)), w_a((D_EXPERT, D_MODEL)),
                      w_b((D_MODEL, D_EXPERT)), w_b((D_MODEL, D_EXPERT)), w_b((D_EXPERT, D_MODEL)),
                      const((1, D_MODEL))],
            out_specs=pl.BlockSpec(memory_space=pl.ANY),
            scratch_shapes=[pltpu.VMEM((2, MOE_TILE, D_MODEL), F32),
                            pltpu.VMEM((MOE_TILE, D_MODEL), BF16),
                            pltpu.VMEM((MOE_TILE, D_MODEL), F32),
                            pltpu.VMEM((SUBLANES, MOE_TILE), I32),
                            pltpu.SMEM((2, MOE_TILE), I32),
                            pltpu.SemaphoreType.DMA(()),
                            pltpu.SemaphoreType.DMA((2,))]),
        out_shape=jax.ShapeDtypeStruct((n_tok, D_MODEL), F32),
        compiler_params=pltpu.CompilerParams(
            dimension_semantics=("arbitrary",), vmem_limit_bytes=VMEM_LIMIT_BYTES),
        name="moe_sorted",
    )(step_blk, step_ea, step_eb, step_nv, n_used, xs, g2, wg, wu, wd, wg, wu, wd, gf)


def _moe_dense_body(x1_ref, route_ref, g2_ref, wg_ref, wu_ref, wd_ref, gf_ref, y_ref, h2_s, acc_s):
    e = pl.program_id(1)

    @pl.when(e == 0)
    def _first():
        h2_s[...] = _rmsnorm(x1_ref[...], g2_ref[...]).astype(BF16)
        acc_s[...] = jnp.zeros(acc_s.shape, F32)

    h2 = h2_s[...]
    a = jnp.dot(h2, wg_ref[...], preferred_element_type=F32)
    u = jnp.dot(h2, wu_ref[...], preferred_element_type=F32)
    y = jnp.dot((_silu(a) * u).astype(BF16), wd_ref[...], preferred_element_type=F32)
    route = route_ref[...]
    lane = lax.broadcasted_iota(I32, route.shape, 1)
    gate = jnp.sum(jnp.where(lane == e + ROUTE_E0, route, 0.0), axis=-1, keepdims=True)
    acc_s[...] += gate * y

    @pl.when(e == N_EXPERTS - 1)
    def _last():
        y_ref[...] = _rmsnorm(x1_ref[...] + acc_s[...], gf_ref[...])


def _moe_dense_call(x1, route, g2, wg, wu, wd, gf, tm):
    n = x1.shape[0]
    return pl.pallas_call(
        _moe_dense_body,
        grid=(n // tm, N_EXPERTS),
        in_specs=[pl.BlockSpec((tm, D_MODEL), lambda i, e: (i, 0)),
                  pl.BlockSpec((tm, LANES), lambda i, e: (i, 0)),
                  pl.BlockSpec((1, D_MODEL), lambda i, e: (0, 0)),
                  pl.BlockSpec((None, D_MODEL, D_EXPERT), lambda i, e: (e, 0, 0)),
                  pl.BlockSpec((None, D_MODEL, D_EXPERT), lambda i, e: (e, 0, 0)),
                  pl.BlockSpec((None, D_EXPERT, D_MODEL), lambda i, e: (e, 0, 0)),
                  pl.BlockSpec((1, D_MODEL), lambda i, e: (0, 0))],
        out_specs=pl.BlockSpec((tm, D_MODEL), lambda i, e: (i, 0)),
        out_shape=jax.ShapeDtypeStruct((n, D_MODEL), F32),
        scratch_shapes=[pltpu.VMEM((tm, D_MODEL), BF16), pltpu.VMEM((tm, D_MODEL), F32)],
        compiler_params=pltpu.CompilerParams(
            dimension_semantics=("arbitrary", "arbitrary"),
            vmem_limit_bytes=VMEM_LIMIT_BYTES),
        name="moe_dense",
    )(x1, route, g2, wg, wu, wd, gf)


def _pick_block(seq, target):
    tb = min(seq, target)
    assert seq % tb == 0 and tb % CHUNK == 0
    return tb


def _lookup(table, idx):
    table = jnp.asarray(table, I32)
    k = jnp.arange(table.shape[0], dtype=I32)
    return jnp.sum(jnp.where(idx[..., None] == k, table, 0), axis=-1).astype(I32)


def _plan_steps(tab, counts, n_tiles):
    tile = jnp.arange(n_tiles, dtype=I32)
    n_used = tab[TAB_USED, 0]
    t_cls = tab[TAB_CLS, :n_tiles]
    t_ord = tab[TAB_ORD, :n_tiles]
    live = tile < n_used
    key = jnp.where(live, t_cls * n_tiles + tile, N_CLASSES * n_tiles + tile)
    pos = jnp.sum(key[None, :] < key[:, None], axis=1).astype(I32)
    step = jnp.arange(n_tiles + 2, dtype=I32)
    step_c = jnp.minimum(step, n_used - 1)
    step_tile = jnp.sum(jnp.where(pos[None, :] == step_c[:, None], tile[None, :], 0), axis=1).astype(I32)
    s_cls = _lookup(t_cls, step_tile)
    s_ord = _lookup(t_ord, step_tile)
    s_nv = jnp.clip(_lookup(counts, s_cls) - s_ord * MOE_TILE, 0, MOE_TILE).astype(I32)
    cls_ea = np.array([g * EXPERTS_PER_GROUP + a for g in range(N_GROUPS) for a, _ in PAIRS], np.int32)
    cls_eb = np.array([g * EXPERTS_PER_GROUP + b for g in range(N_GROUPS) for _, b in PAIRS], np.int32)
    return step_tile, _lookup(cls_ea, s_cls), _lookup(cls_eb, s_cls), s_nv, n_used.reshape(1).astype(I32)


def kernel(x_prompt, x_sample, cache_k, cache_v, state_hgrn, norm1_g, w_in, attn_sink, lower_bounds,
           hgrn_norm_g, w_o, norm2_g, w_router_group, b_router_group, w_router_expert, b_router_expert,
           w_gate, w_up, w_down, final_norm_g):
    depth = w_in.shape[0]
    assert depth == 1
    l = 0
    w_hist = cache_k.shape[2]
    assert w_hist == WINDOW
    w_r = jnp.concatenate(
        [w_router_group[l], jnp.transpose(w_router_expert[l], (1, 0, 2)).reshape(D_MODEL, N_EXPERTS)], axis=1)
    w_r = jnp.pad(w_r, ((0, 0), (0, CLS_ROWS - w_r.shape[1]))).T.astype(BF16)
    b_r = jnp.concatenate([b_router_group[l], b_router_expert[l].reshape(N_EXPERTS)])
    b_r = jnp.pad(b_r, (0, CLS_ROWS - b_r.shape[0])).reshape(CLS_ROWS, 1).astype(F32)
    weights = (norm1_g[l].reshape(1, D_MODEL), w_in[l].astype(BF16), attn_sink[l].astype(F32),
               lower_bounds.astype(F32), hgrn_norm_g[l].reshape(1, B_WIDTH), w_o[l].astype(BF16),
               norm2_g[l].reshape(1, D_MODEL), w_r, b_r)
    g2 = norm2_g[l].reshape(1, D_MODEL)
    gf = final_norm_g.reshape(1, D_MODEL)
    wg, wu, wd = w_gate[l].astype(BF16), w_up[l].astype(BF16), w_down[l].astype(BF16)

    bp, lp, _ = x_prompt.shape
    bs, ls, _ = x_sample.shape
    np_tok, ns_tok = bp * lp, bs * ls
    tbp, tbs = _pick_block(lp, LAYER_BLOCK), _pick_block(ls, LAYER_BLOCK)
    assert tbp % MOE_TILE == 0 and np_tok % MOE_TILE == 0 and np_tok < TOK_SPLIT * TOK_SPLIT

    n_tiles = np_tok // MOE_TILE + N_CLASSES
    assert n_tiles <= TILE_LANES
    xs, tab, cnt, kp, vp, sp = _layer_call(x_prompt, None, weights, 0, tbp, n_tiles + tbp // MOE_TILE)
    counts = cnt[:N_CLASSES, 0].astype(I32)
    step_blk, step_ea, step_eb, step_nv, n_used = _plan_steps(tab, counts, n_tiles)
    yp = _moe_sorted_call(step_blk, step_ea, step_eb, step_nv, n_used, xs, g2, wg, wu, wd, gf, np_tok)

    cache = (cache_k[l].reshape(bs, w_hist, A_KV_WIDTH), cache_v[l].reshape(bs, w_hist, A_KV_WIDTH), state_hgrn[l])
    x1s, routes, kn, vn, sn = _layer_call(x_sample, cache, weights, PAST_LEN, tbs)
    ys = _moe_dense_call(x1s.reshape(ns_tok, D_MODEL), routes.reshape(ns_tok, LANES), g2, wg, wu, wd, gf,
                         _pick_block(ns_tok, LAYER_BLOCK))

    kv_shape = (1, -1, w_hist, A_KV_HEADS, A_HEAD_DIM)
    return (yp.reshape(bp, lp, D_MODEL), ys.reshape(bs, ls, D_MODEL),
            kp.reshape(kv_shape), vp.reshape(kv_shape), sp[None],
            kn.reshape(kv_shape), vn.reshape(kv_shape), sn[None])
```

```python
import functools

import numpy as np
import jax
import jax.numpy as jnp
from jax import lax
from jax.experimental import pallas as pl
from jax.experimental.pallas import tpu as pltpu

F32 = jnp.float32
BF16 = jnp.bfloat16
I32 = jnp.int32

D_MODEL = 1024
CHUNK = 64
EPS = 1e-6
PAST_LEN = 4096
WINDOW = 128
A_HEADS = 8
A_KV_HEADS = 2
A_HEAD_DIM = 64
A_GROUP = A_HEADS // A_KV_HEADS
A_WIDTH = A_HEADS * A_HEAD_DIM
A_KV_WIDTH = A_KV_HEADS * A_HEAD_DIM
B_HEADS = 4
B_KEY_DIM = 128
B_VAL_DIM = 128
B_KEY_WIDTH = B_HEADS * B_KEY_DIM
B_WIDTH = B_HEADS * B_VAL_DIM
MIX_WIDTH = A_WIDTH + B_WIDTH
OFF_K = A_WIDTH
OFF_V = OFF_K + A_KV_WIDTH
OFF_QB = OFF_V + A_KV_WIDTH
OFF_FB = OFF_QB + B_KEY_WIDTH
OFF_IB = OFF_FB + B_KEY_WIDTH
OFF_GB = OFF_IB + B_WIDTH
N_IN = OFF_GB + B_WIDTH
N_GROUPS = 4
EXPERTS_PER_GROUP = 4
N_EXPERTS = N_GROUPS * EXPERTS_PER_GROUP
D_EXPERT = 256
LANES = 128
SUBLANES = 8
ROUTE_E0 = N_GROUPS
PAIRS = [(a, b) for a in range(EXPERTS_PER_GROUP) for b in range(a + 1, EXPERTS_PER_GROUP)]
N_PAIRS = len(PAIRS)
N_CLASSES = N_GROUPS * N_PAIRS
CLS_ROWS = 32
ROW_WIDTH = D_MODEL + LANES
META_GA, META_GB, META_TOK_HI, META_TOK_LO = 0, 1, 2, 3
TOK_SPLIT = 256
LAYER_BLOCK = 512
MOE_TILE = 256
MOE_TILE_SHIFT = 8
TILE_LANES = 256
TAB_CLS, TAB_ORD, TAB_USED = 0, 1, 2

VMEM_LIMIT_BYTES = 56 * 1024 * 1024

NT_DIMS = (((1,), (1,)), ((), ()))
TN_DIMS = (((0,), (0,)), ((), ()))


def _rmsnorm(x, g):
    return x * lax.rsqrt(jnp.mean(x * x, axis=-1, keepdims=True) + EPS) * g


def _silu(x):
    hx = 0.5 * x
    return hx + hx * jnp.tanh(hx)


def _route(lt):
    t = lt.shape[1]
    neg = -jnp.inf
    assert N_GROUPS == EXPERTS_PER_GROUP
    r4 = lax.broadcasted_iota(I32, (N_GROUPS, t), 0).astype(F32)

    def first_row(mask):
        return jnp.min(jnp.where(mask, r4, float(N_GROUPS)), axis=0, keepdims=True)

    gl = lt[0:N_GROUPS]
    gmax = jnp.max(gl, axis=0, keepdims=True)
    gidx = first_row(gl == gmax)
    p_group = 1.0 / jnp.sum(jnp.exp(gl - gmax), axis=0, keepdims=True)
    el = lt[ROUTE_E0:ROUTE_E0 + EXPERTS_PER_GROUP]
    for g in range(1, N_GROUPS):
        lo = ROUTE_E0 + g * EXPERTS_PER_GROUP
        el = jnp.where(gidx == g, lt[lo:lo + EXPERTS_PER_GROUP], el)
    e1 = jnp.max(el, axis=0, keepdims=True)
    i1 = first_row(el == e1)
    el2 = jnp.where(r4 == i1, neg, el)
    e2 = jnp.max(el2, axis=0, keepdims=True)
    i2 = first_row(el2 == e2)
    tt = jnp.exp(e2 - e1)
    w1 = p_group / (1.0 + tt)
    w2 = w1 * tt
    first_low = i1 < i2
    ea = jnp.where(first_low, i1, i2)
    eb = jnp.where(first_low, i2, i1)
    ga = jnp.where(first_low, w1, w2)
    gb = jnp.where(first_low, w2, w1)
    pair = 0.5 * (ea * (2 * EXPERTS_PER_GROUP - 1 - ea)) + eb - ea - 1.0
    rows = lax.broadcasted_iota(I32, (CLS_ROWS, t), 0).astype(F32)
    e_row0 = ROUTE_E0 + EXPERTS_PER_GROUP * gidx
    dense = jnp.where(rows == e_row0 + i1, w1, 0.0) + jnp.where(rows == e_row0 + i2, w2, 0.0)
    return gidx * N_PAIRS + pair, ga, gb, dense


def _rows_to_lanes(rows_t):
    r, t = rows_t.shape
    padded = jnp.concatenate([rows_t, jnp.zeros((LANES - r, t), F32)], axis=0)
    return padded.T


def _mixer_and_router(pos0, tb, j, x_ref, g1_ref, win_ref, sink_ref, lb_ref, hg_ref, wo_ref, g2_ref, wr_ref, br_ref,
                      proj_s, kbuf, vbuf, st_s, mix_s, sc_s, cum_s, kf_s, qf_s, qcum_s, dec_s, a_s, ds_s):
    nc = tb // CHUNK
    nk = WINDOW + CHUNK
    x = x_ref[...]
    h = _rmsnorm(x, g1_ref[...]).astype(BF16)
    proj_s[...] = jnp.dot(h, win_ref[...], preferred_element_type=F32)
    kbuf[WINDOW:WINDOW + tb, :] = proj_s[:, OFF_K:OFF_K + A_KV_WIDTH]
    vbuf[WINDOW:WINDOW + tb, :] = proj_s[:, OFF_V:OFF_V + A_KV_WIDTH]

    row = lax.broadcasted_iota(I32, (A_GROUP * CHUNK, nk), 0)
    col = lax.broadcasted_iota(I32, (A_GROUP * CHUNK, nk), 1)
    dist = jnp.abs((row & (CHUNK - 1)) - (col - WINDOW)).astype(F32)
    row_head = row // CHUNK
    rowc_head = lax.broadcasted_iota(I32, (A_GROUP * CHUNK, 1), 0) // CHUNK
    bias = []
    sinkc = []
    for hk in range(A_KV_HEADS):
        slope = jnp.exp2(-(row_head + (hk * A_GROUP + 1)).astype(F32))
        bias.append(slope * dist)
        sc = jnp.zeros((A_GROUP * CHUNK, 1), F32)
        for g in range(A_GROUP):
            sc = jnp.where(rowc_head == g, sink_ref[hk * A_GROUP + g], sc)
        sinkc.append(sc)

    lbr = lb_ref[...]
    lbm = jnp.max(lbr, axis=0, keepdims=True)
    lbe = jnp.exp(lbr - lbm)
    lb = lbe[0:1, :] / jnp.sum(lbe, axis=0, keepdims=True)
    f_mid = 0.5 * (1.0 + lb)
    f_half = 0.5 * (1.0 - lb)
    hg = hg_ref[...]
    tr = lax.broadcasted_iota(I32, (CHUNK, CHUNK), 0)
    tc = lax.broadcasted_iota(I32, (CHUNK, CHUNK), 1)
    tril = tr >= tc
    ltri = jnp.where(tril, 1.0, 0.0).astype(BF16)

    def chunk_rows(c):
        r0 = pl.multiple_of(c * CHUNK, CHUNK)
        return r0, pl.ds(r0, CHUNK)

    def stage_scores(c, par):
        r0, rows = chunk_rows(c)
        for hk in range(A_KV_HEADS):
            qc = proj_s[rows, hk * A_GROUP * A_HEAD_DIM:(hk + 1) * A_GROUP * A_HEAD_DIM] * (A_HEAD_DIM ** -0.5)
            q4 = jnp.concatenate([qc[:, g * A_HEAD_DIM:(g + 1) * A_HEAD_DIM] for g in range(A_GROUP)],
                                 axis=0).astype(BF16)
            kw = kbuf[pl.ds(r0, nk), hk * A_HEAD_DIM:(hk + 1) * A_HEAD_DIM].astype(BF16)
            sc_s[par * A_KV_HEADS + hk] = lax.dot_general(q4, kw, NT_DIMS, preferred_element_type=F32)
        qb = proj_s[rows, OFF_QB:OFF_QB + B_KEY_WIDTH]
        fl = proj_s[rows, OFF_FB:OFF_FB + B_KEY_WIDTH]
        f = f_mid + f_half * jnp.tanh(0.5 * fl)
        logf = jnp.log(f)
        hi = logf.astype(BF16)
        lo = (logf - hi.astype(F32)).astype(BF16)
        cum_s[par] = (jnp.dot(ltri, hi, preferred_element_type=F32)
                      + jnp.dot(ltri, lo, preferred_element_type=F32))
        kf_s[par] = 1.0 - f
        qf_s[par] = _silu(qb)

    def stage_softmax(c, par):
        r0, rows = chunk_rows(c)
        valid = col >= (WINDOW - pos0) - (j * tb + c * CHUNK)
        for hk in range(A_KV_HEADS):
            vw = vbuf[pl.ds(r0, nk), hk * A_HEAD_DIM:(hk + 1) * A_HEAD_DIM].astype(BF16)
            s = jnp.where(valid, sc_s[par * A_KV_HEADS + hk] - bias[hk], -jnp.inf)
            m = jnp.maximum(jnp.max(s, axis=-1, keepdims=True), sinkc[hk])
            p = jnp.exp(s - m)
            den = jnp.sum(p, axis=-1, keepdims=True) + jnp.exp(sinkc[hk] - m)
            o = jnp.dot(p.astype(BF16), vw, preferred_element_type=F32) / den
            att = jnp.concatenate([o[g * CHUNK:(g + 1) * CHUNK] for g in range(A_GROUP)], axis=1)
            mix_s[rows, hk * A_GROUP * A_HEAD_DIM:(hk + 1) * A_GROUP * A_HEAD_DIM] = att.astype(BF16)
        cum = cum_s[par]
        qf = qf_s[par]
        kf = kf_s[par]
        vb = proj_s[rows, OFF_IB:OFF_IB + B_WIDTH].astype(BF16)
        ref = cum[CHUNK // 2:CHUNK // 2 + 1, :]
        tot = cum[CHUNK - 1:CHUNK, :]
        e_fwd = jnp.exp(cum - ref)
        e_bwd = 1.0 / e_fwd
        q_rel = qf * e_fwd
        k_rel = kf * e_bwd
        q_in = q_rel.astype(BF16)
        k_in = k_rel.astype(BF16)
        k_tot = (k_rel * jnp.exp(tot - ref)).astype(BF16)
        qcum_s[par] = (q_rel * jnp.exp(ref)).astype(BF16)
        dec_s[par] = jnp.exp(tot)
        for hh in range(B_HEADS):
            sl = slice(hh * B_KEY_DIM, (hh + 1) * B_KEY_DIM)
            a = lax.dot_general(q_in[:, sl], k_in[:, sl], NT_DIMS, preferred_element_type=F32)
            a_s[par * B_HEADS + hh] = jnp.where(tril, a, 0.0).astype(BF16)
            ds_s[par * B_HEADS + hh] = lax.dot_general(vb[:, sl], k_tot[:, sl], TN_DIMS,
                                                       preferred_element_type=F32)

    def stage_state(c, par):
        r0, rows = chunk_rows(c)
        vb = proj_s[rows, OFF_IB:OFF_IB + B_WIDTH].astype(BF16)
        gg = proj_s[rows, OFF_GB:OFF_GB + B_WIDTH]
        dec = dec_s[par]
        outs = []
        for hh in range(B_HEADS):
            sl = slice(hh * B_KEY_DIM, (hh + 1) * B_KEY_DIM)
            st = st_s[hh]
            o = (jnp.dot(a_s[par * B_HEADS + hh], vb[:, sl], preferred_element_type=F32)
                 + lax.dot_general(qcum_s[par, :, sl], st.astype(BF16), NT_DIMS, preferred_element_type=F32))
            st_s[hh] = st * dec[:, sl] + ds_s[par * B_HEADS + hh]
            o = o * lax.rsqrt(jnp.mean(o * o, axis=-1, keepdims=True) + EPS) * hg[:, sl]
            outs.append(o)
        rec = jnp.concatenate(outs, axis=1) * _silu(gg)
        mix_s[rows, A_WIDTH:A_WIDTH + B_WIDTH] = rec.astype(BF16)

    if nc >= 4 and nc % 2 == 0:
        stage_scores(0, 0)
        stage_scores(1, 1)
        stage_softmax(0, 0)

        def pipelined(k, carry):
            for u in range(2):
                i = 2 * k + u
                stage_scores(i + 2, u)
                stage_softmax(i + 1, 1 - u)
                stage_state(i, u)
            return carry

        lax.fori_loop(0, (nc - 2) // 2, pipelined, 0)
        stage_softmax(nc - 1, 1)
        stage_state(nc - 2, 0)
        stage_state(nc - 1, 1)
    else:
        for c in range(nc):
            stage_scores(c, c % 2)
            stage_softmax(c, c % 2)
            stage_state(c, c % 2)

    x1 = x + jnp.dot(mix_s[...], wo_ref[...], preferred_element_type=F32)
    h2 = _rmsnorm(x1, g2_ref[...]).astype(BF16)
    logits_t = lax.dot_general(wr_ref[...], h2, NT_DIMS, preferred_element_type=F32) + br_ref[...]
    return x1, logits_t


def _row_copy(src_ref, src_row, dst_ref, dst_row, sem):
    return pltpu.make_async_copy(src_ref.at[pl.ds(src_row, 1)], dst_ref.at[pl.ds(dst_row, 1)], sem)


def _issue_rows(n, src_ref, src_slot, dst_ref, idx_ref, sem, first=0):
    for r in range(first, n):
        _row_copy(src_ref.at[src_slot], r, dst_ref, idx_ref[src_slot, r], sem.at[src_slot]).start(priority=r % 2)


def _layer_body(pos0, tb, sorted_out, *refs):
    if sorted_out:
        (x_ref, g1_ref, win_ref, sink_ref, lb_ref, hg_ref, wo_ref, g2_ref, wr_ref, br_ref,
         xs_ref, tab_ref, cnt_ref, kwin_ref, vwin_ref, sout_ref,
         proj_s, kbuf, vbuf, st_s, mix_s, *pipe_s, xrow_s, cnt_s, cur_s, tab_s, before_s, dest_v, dest_sm, dsem, rsem) = refs
    else:
        (x_ref, ck_ref, cv_ref, s0_ref, g1_ref, win_ref, sink_ref, lb_ref, hg_ref, wo_ref, g2_ref, wr_ref, br_ref,
         x1_ref, route_ref, kwin_ref, vwin_ref, sout_ref,
         proj_s, kbuf, vbuf, st_s, mix_s, *pipe_s) = refs
    b = pl.program_id(0)
    j = pl.program_id(1)
    nblk = pl.num_programs(1)
    step = b * nblk + j

    if sorted_out:
        slot = step % 2
        prev = 1 - slot
        n_tiles = xs_ref.shape[0] // MOE_TILE
        spare_row0 = (n_tiles - tb // MOE_TILE) * MOE_TILE

        @pl.when(step == 0)
        def _init_routing():
            cnt_s[...] = jnp.zeros(cnt_s.shape, F32)
            cur_s[...] = jnp.zeros(cur_s.shape, F32)
            tab_s[...] = jnp.zeros(tab_s.shape, I32)
            ur = lax.broadcasted_iota(I32, (tb, tb), 0)
            uc = lax.broadcasted_iota(I32, (tb, tb), 1)
            before_s[...] = jnp.where(ur < uc, 1.0, 0.0).astype(BF16)
            xrow_s[1] = jnp.zeros(xrow_s.shape[1:], F32)

            def fill(r, carry):
                dest_sm[1, r] = spare_row0 + r
                return carry

            lax.fori_loop(0, tb, fill, 0)

        def dest_copy(s):
            return pltpu.make_async_copy(dest_v.at[pl.ds(0, 1)], dest_sm.at[pl.ds(s, 1)], dsem)

        @pl.when(step > 0)
        def _dest_landed():
            dest_copy(prev).wait()

    @pl.when(j == 0)
    def _init_stream():
        if sorted_out:
            kbuf[0:WINDOW, :] = jnp.zeros((WINDOW, A_KV_WIDTH), F32)
            vbuf[0:WINDOW, :] = jnp.zeros((WINDOW, A_KV_WIDTH), F32)
            st_s[...] = jnp.zeros(st_s.shape, F32)
        else:
            kbuf[0:WINDOW, :] = ck_ref[...]
            vbuf[0:WINDOW, :] = cv_ref[...]
            for hh in range(B_HEADS):
                st_s[hh] = s0_ref[hh].T

    if sorted_out:
        _issue_rows(tb, xrow_s, prev, xs_ref, dest_sm, rsem)

    x1, logits_t = _mixer_and_router(pos0, tb, j, x_ref, g1_ref, win_ref, sink_ref, lb_ref, hg_ref, wo_ref, g2_ref,
                                   wr_ref, br_ref, proj_s, kbuf, vbuf, st_s, mix_s, *pipe_s)
    cls, ga, gb, dense = _route(logits_t)

    if not sorted_out:
        x1_ref[...] = x1
        route_ref[...] = _rows_to_lanes(dense)
    else:
        tok = step * tb + lax.broadcasted_iota(I32, (1, tb), 1)
        tok_hi = (tok >> 8).astype(F32)
        tok_lo = (tok & (TOK_SPLIT - 1)).astype(F32)
        assert (META_GA, META_GB, META_TOK_HI, META_TOK_LO) == (0, 1, 2, 3)
        xrow_s[slot, :, 0:D_MODEL] = x1
        xrow_s[slot, :, D_MODEL:ROW_WIDTH] = _rows_to_lanes(jnp.concatenate([ga, gb, tok_hi, tok_lo], axis=0))

        oht = jnp.where(lax.broadcasted_iota(I32, (CLS_ROWS, tb), 0).astype(F32) == cls, 1.0, 0.0)
        prefix = jnp.dot(oht.astype(BF16), before_s[...], preferred_element_type=F32)
        cnt = cnt_s[:, 0:1]
        rank = jnp.sum(oht * (prefix + cnt), axis=0, keepdims=True).astype(I32)

        n_c = jnp.sum(oht, axis=1, keepdims=True)
        cnt_i = cnt.astype(I32)
        after_i = (cnt + n_c).astype(I32)
        tiles_before = (cnt_i + (MOE_TILE - 1)) >> MOE_TILE_SHIFT
        new_c = ((after_i + (MOE_TILE - 1)) >> MOE_TILE_SHIFT) - tiles_before
        sr = lax.broadcasted_iota(I32, (CLS_ROWS, CLS_ROWS), 0)
        sc = lax.broadcasted_iota(I32, (CLS_ROWS, CLS_ROWS), 1)
        lower = jnp.where(sr > sc, 1.0, 0.0).astype(BF16)
        new_b = jnp.broadcast_to(new_c.astype(F32), (CLS_ROWS, LANES)).astype(BF16)
        opened_before = jnp.dot(lower, new_b, preferred_element_type=F32)[:, 0:1].astype(I32)
        used = tab_s[TAB_USED:TAB_USED + 1, 0:1]
        base_c = used + opened_before
        partial = (cnt_i & (MOE_TILE - 1)) != 0
        cur_c = cur_s[:, 0:1].astype(I32)
        first_c = jnp.where(partial, cur_c, base_c)
        shift_c = jnp.where(partial, base_c - 1, base_c)
        ord0_c = cnt_i >> MOE_TILE_SHIFT

        def per_token(col):
            return jnp.sum(oht * col.astype(F32), axis=0, keepdims=True).astype(I32)

        ord_t = rank >> MOE_TILE_SHIFT
        d_t = ord_t - per_token(ord0_c)
        tile_t = jnp.where(d_t == 0, per_token(first_c), per_token(shift_c) + d_t)
        dest = tile_t * MOE_TILE + (rank & (MOE_TILE - 1))

        tl = lax.broadcasted_iota(I32, (CLS_ROWS, TILE_LANES), 1)
        opened = (tl >= base_c) & (tl < base_c + new_c)
        cls_id = lax.broadcasted_iota(I32, (CLS_ROWS, TILE_LANES), 0)
        hit = jnp.sum(jnp.where(opened, 1, 0), axis=0, keepdims=True) > 0
        t_cls = jnp.sum(jnp.where(opened, cls_id, 0), axis=0, keepdims=True)
        t_ord = jnp.sum(jnp.where(opened, tiles_before + (tl - base_c), 0), axis=0, keepdims=True)
        tab_s[TAB_CLS:TAB_CLS + 1, :] = jnp.where(hit, t_cls, tab_s[TAB_CLS:TAB_CLS + 1, :])
        tab_s[TAB_ORD:TAB_ORD + 1, :] = jnp.where(hit, t_ord, tab_s[TAB_ORD:TAB_ORD + 1, :])
        used_new = used + jnp.sum(new_c, axis=0, keepdims=True)
        tab_s[TAB_USED:TAB_USED + 1, :] = jnp.broadcast_to(used_new, (1, TILE_LANES))
        cur_s[...] = jnp.broadcast_to(jnp.where(new_c > 0, base_c + new_c - 1, cur_c).astype(F32), cur_s.shape)
        cnt_s[...] = jnp.broadcast_to(cnt + n_c, cnt_s.shape)
        tab_ref[...] = tab_s[...]
        cnt_ref[...] = cnt_s[...]

        dest_v[...] = jnp.broadcast_to(dest, dest_v.shape)
        dest_copy(slot).start()

        def wait_rows(s):
            pltpu.make_async_copy(xrow_s.at[s], xs_ref.at[pl.ds(0, tb)], rsem.at[s]).wait()

        wait_rows(prev)

        @pl.when(step == pl.num_programs(0) * nblk - 1)
        def _flush():
            dest_copy(slot).wait()
            _issue_rows(tb, xrow_s, slot, xs_ref, dest_sm, rsem)
            wait_rows(slot)

    kt = kbuf[tb:tb + WINDOW, :]
    vt = vbuf[tb:tb + WINDOW, :]
    kbuf[0:WINDOW, :] = kt
    vbuf[0:WINDOW, :] = vt

    @pl.when(j == nblk - 1)
    def _stream_outputs():
        kwin_ref[...] = kbuf[0:WINDOW, :]
        vwin_ref[...] = vbuf[0:WINDOW, :]
        for hh in range(B_HEADS):
            sout_ref[hh] = st_s[hh].T


def _layer_call(x, cache, weights, pos0, tb, n_sorted_tiles=None):
    bsz, seq, _ = x.shape
    nblk = seq // tb
    sorted_out = cache is None
    g1, w_in, sink, lower_bounds, hg, w_o, g2, w_r, b_r = weights

    def const(shape):
        return pl.BlockSpec(shape, lambda b, j: (0,) * len(shape))

    def per_stream(shape):
        return pl.BlockSpec((None,) + shape, lambda b, j: (b,) + (0,) * len(shape))

    in_specs = [pl.BlockSpec((None, tb, D_MODEL), lambda b, j: (b, j, 0))]
    args = [x]
    if not sorted_out:
        in_specs += [per_stream((WINDOW, A_KV_WIDTH)), per_stream((WINDOW, A_KV_WIDTH)),
                     per_stream((B_HEADS, B_KEY_DIM, B_VAL_DIM))]
        args += list(cache)
    in_specs += [const((1, D_MODEL)), const((D_MODEL, N_IN)),
                 pl.BlockSpec(memory_space=pltpu.SMEM),
                 const((2, B_KEY_WIDTH)), const((1, B_WIDTH)), const((MIX_WIDTH, D_MODEL)),
                 const((1, D_MODEL)), const((CLS_ROWS, D_MODEL)), const((CLS_ROWS, 1))]
    args += [g1, w_in, sink, lower_bounds, hg, w_o, g2, w_r, b_r]
    stream_shapes = (jax.ShapeDtypeStruct((bsz, WINDOW, A_KV_WIDTH), F32),
                     jax.ShapeDtypeStruct((bsz, WINDOW, A_KV_WIDTH), F32),
                     jax.ShapeDtypeStruct((bsz, B_HEADS, B_KEY_DIM, B_VAL_DIM), F32))
    stream_specs = (per_stream((WINDOW, A_KV_WIDTH)), per_stream((WINDOW, A_KV_WIDTH)),
                    per_stream((B_HEADS, B_KEY_DIM, B_VAL_DIM)))
    scratch = [pltpu.VMEM((tb, N_IN), F32),
               pltpu.VMEM((WINDOW + tb, A_KV_WIDTH), F32),
               pltpu.VMEM((WINDOW + tb, A_KV_WIDTH), F32),
               pltpu.VMEM((B_HEADS, B_VAL_DIM, B_KEY_DIM), F32),
               pltpu.VMEM((tb, MIX_WIDTH), BF16),
               pltpu.VMEM((2 * A_KV_HEADS, A_GROUP * CHUNK, WINDOW + CHUNK), F32),
               pltpu.VMEM((2, CHUNK, B_KEY_WIDTH), F32),
               pltpu.VMEM((2, CHUNK, B_KEY_WIDTH), F32),
               pltpu.VMEM((2, CHUNK, B_KEY_WIDTH), F32),
               pltpu.VMEM((2, CHUNK, B_KEY_WIDTH), BF16),
               pltpu.VMEM((2, 1, B_KEY_WIDTH), F32),
               pltpu.VMEM((2 * B_HEADS, CHUNK, CHUNK), BF16),
               pltpu.VMEM((2 * B_HEADS, B_VAL_DIM, B_KEY_DIM), F32)]
    if sorted_out:
        out_shape = (jax.ShapeDtypeStruct((n_sorted_tiles * MOE_TILE, ROW_WIDTH), F32),
                     jax.ShapeDtypeStruct((SUBLANES, TILE_LANES), I32),
                     jax.ShapeDtypeStruct((CLS_ROWS, LANES), F32)) + stream_shapes
        out_specs = (pl.BlockSpec(memory_space=pl.ANY), const((SUBLANES, TILE_LANES)),
                     const((CLS_ROWS, LANES))) + stream_specs
        scratch += [pltpu.VMEM((2, tb, ROW_WIDTH), F32),
                    pltpu.VMEM((CLS_ROWS, LANES), F32),
                    pltpu.VMEM((CLS_ROWS, LANES), F32),
                    pltpu.VMEM((SUBLANES, TILE_LANES), I32),
                    pltpu.VMEM((tb, tb), BF16),
                    pltpu.VMEM((SUBLANES, tb), I32),
                    pltpu.SMEM((2, tb), I32),
                    pltpu.SemaphoreType.DMA(()),
                    pltpu.SemaphoreType.DMA((2,))]
    else:
        out_shape = (jax.ShapeDtypeStruct((bsz, seq, D_MODEL), F32),
                     jax.ShapeDtypeStruct((bsz, seq, LANES), F32)) + stream_shapes
        out_specs = (pl.BlockSpec((None, tb, D_MODEL), lambda b, j: (b, j, 0)),
                     pl.BlockSpec((None, tb, LANES), lambda b, j: (b, j, 0))) + stream_specs
    return pl.pallas_call(
        functools.partial(_layer_body, pos0, tb, sorted_out),
        grid=(bsz, nblk),
        in_specs=in_specs,
        out_specs=out_specs,
        out_shape=out_shape,
        scratch_shapes=scratch,
        compiler_params=pltpu.CompilerParams(
            dimension_semantics=("arbitrary", "arbitrary"),
            vmem_limit_bytes=VMEM_LIMIT_BYTES),
        name="layer_prompt" if sorted_out else "layer_sample",
    )(*args)


def _moe_sorted_body(blk_ref, ea_ref, eb_ref, nv_ref, nused_ref, xs_ref, g2_ref,
                     wga_ref, wua_ref, wda_ref, wgb_ref, wub_ref, wdb_ref, gf_ref, y_ref,
                     ybuf, h2_s, ya_s, tok_v, tok_sm, tsem, rsem):
    p = pl.program_id(0)
    n_used = nused_ref[0]
    slot = p % 2
    prev = 1 - slot
    nv_prev = nv_ref[jnp.maximum(p - 1, 0)]
    nv_prev2 = nv_ref[jnp.maximum(p - 2, 0)]

    def expert(h2, wg_ref, wu_ref, wd_ref):
        a = jnp.dot(h2, wg_ref[...].astype(BF16), preferred_element_type=F32)
        u = jnp.dot(h2, wu_ref[...].astype(BF16), preferred_element_type=F32)
        return jnp.dot((_silu(a) * u).astype(BF16), wd_ref[...].astype(BF16), preferred_element_type=F32)

    def tok_copy():
        return pltpu.make_async_copy(tok_v.at[pl.ds(0, 1)], tok_sm.at[pl.ds(slot, 1)], tsem)

    def first_half():
        meta = xs_ref[:, D_MODEL:ROW_WIDTH]
        sr = lax.broadcasted_iota(I32, (SUBLANES, LANES), 0)
        sc = lax.broadcasted_iota(I32, (SUBLANES, LANES), 1)
        pick = jnp.where(sc == sr + META_TOK_HI, 1.0, 0.0).astype(BF16)
        ids = lax.dot_general(pick, meta.astype(BF16), NT_DIMS, preferred_element_type=F32)
        tok = (ids[0:1, :] * TOK_SPLIT + ids[1:2, :]).astype(I32)
        tok_v[...] = jnp.broadcast_to(tok, tok_v.shape)
        tok_copy().start()
        h2 = _rmsnorm(xs_ref[:, 0:D_MODEL], g2_ref[...]).astype(BF16)
        h2_s[...] = h2
        ya_s[...] = expert(h2, wga_ref, wua_ref, wda_ref)

    def second_half():
        x1 = xs_ref[:, 0:D_MODEL]
        meta = xs_ref[:, D_MODEL:ROW_WIDTH]
        yb = expert(h2_s[...], wgb_ref, wub_ref, wdb_ref)
        moe = meta[:, META_GA:META_GA + 1] * ya_s[...] + meta[:, META_GB:META_GB + 1] * yb
        ybuf[slot] = _rmsnorm(x1 + moe, gf_ref[...])
        tok_copy().wait()

    def send_full():
        _issue_rows(MOE_TILE // 2, ybuf, prev, y_ref, tok_sm, rsem)

    def send_full_rest():
        _issue_rows(MOE_TILE, ybuf, prev, y_ref, tok_sm, rsem, first=MOE_TILE // 2)

    def send_part():
        def start(r, carry):
            _row_copy(ybuf.at[prev], r, y_ref, tok_sm[prev, r], rsem.at[prev]).start()
            return carry

        lax.fori_loop(0, nv_prev, start, 0)

    def wait_full():
        pltpu.make_async_copy(ybuf.at[slot], y_ref.at[pl.ds(0, MOE_TILE)], rsem.at[slot]).wait()

    def wait_part():
        def wait(r, carry):
            _row_copy(ybuf.at[slot], r, y_ref, 0, rsem.at[slot]).wait()
            return carry

        lax.fori_loop(0, nv_prev2, wait, 0)

    has_prev = (p >= 1) & (p <= n_used)
    prev_full = nv_prev == MOE_TILE
    prev_part = jnp.logical_not(prev_full)
    has_prev2 = (p >= 2) & (p <= n_used + 1)
    prev2_full = nv_prev2 == MOE_TILE
    active = p < n_used
    idle = jnp.logical_not(active)
    steady = active & has_prev & prev_full

    @pl.when(steady)
    def _steady_first():
        send_full()
        first_half()

    @pl.when(active & has_prev & prev_part)
    def _after_partial():
        send_part()
        first_half()

    @pl.when(active & jnp.logical_not(has_prev))
    def _first():
        first_half()

    @pl.when(idle & has_prev & prev_full)
    def _flush_full():
        send_full()
        send_full_rest()

    @pl.when(idle & has_prev & prev_part)
    def _flush_part():
        send_part()

    @pl.when(has_prev2 & prev2_full)
    def _wait_full():
        wait_full()

    @pl.when(has_prev2 & jnp.logical_not(prev2_full))
    def _wait_part():
        wait_part()

    @pl.when(steady)
    def _steady_second():
        send_full_rest()
        second_half()

    @pl.when(active & jnp.logical_not(steady))
    def _second():
        second_half()


def _moe_sorted_call(step_blk, step_ea, step_eb, step_nv, n_used, xs, g2, wg, wu, wd, gf, n_tok):
    n_steps = step_blk.shape[0]

    def const(shape):
        return pl.BlockSpec(shape, lambda p, blk, ea, eb, nv, nu: (0,) * len(shape))

    def w_a(shape):
        return pl.BlockSpec((None,) + shape, lambda p, blk, ea, eb, nv, nu: (ea[p], 0, 0))

    def w_b(shape):
        return pl.BlockSpec((None,) + shape, lambda p, blk, ea, eb, nv, nu: (eb[p], 0, 0))

    return pl.pallas_call(
        _moe_sorted_body,
        grid_spec=pltpu.PrefetchScalarGridSpec(
            num_scalar_prefetch=5,
            grid=(n_steps,),
            in_specs=[pl.BlockSpec((MOE_TILE, ROW_WIDTH), lambda p, blk, ea, eb, nv, nu: (blk[p], 0)),
                      const((1, D_MODEL)),
                      w_a((D_MODEL, D_EXPERT)), w_a((D_MODEL, D_EXPERT)), w_a((D_EXPERT, D_MODEL)),
                      w_b((D_MODEL, D_EXPERT)), w_b((D_MODEL, D_EXPERT)), w_b((D_EXPERT, D_MODEL)),
                      const((1, D_MODEL))],
            out_specs=pl.BlockSpec(memory_space=pl.ANY),
            scratch_shapes=[pltpu.VMEM((2, MOE_TILE, D_MODEL), F32),
                            pltpu.VMEM((MOE_TILE, D_MODEL), BF16),
                            pltpu.VMEM((MOE_TILE, D_MODEL), F32),
                            pltpu.VMEM((SUBLANES, MOE_TILE), I32),
                            pltpu.SMEM((2, MOE_TILE), I32),
                            pltpu.SemaphoreType.DMA(()),
                            pltpu.SemaphoreType.DMA((2,))]),
        out_shape=jax.ShapeDtypeStruct((n_tok, D_MODEL), F32),
        compiler_params=pltpu.CompilerParams(
            dimension_semantics=("arbitrary",), vmem_limit_bytes=VMEM_LIMIT_BYTES),
        name="moe_sorted",
    )(step_blk, step_ea, step_eb, step_nv, n_used, xs, g2, wg, wu, wd, wg, wu, wd, gf)


def _moe_dense_body(x1_ref, route_ref, g2_ref, wg_ref, wu_ref, wd_ref, gf_ref, y_ref, h2_s, acc_s):
    e = pl.program_id(1)

    @pl.when(e == 0)
    def _first():
        h2_s[...] = _rmsnorm(x1_ref[...], g2_ref[...]).astype(BF16)
        acc_s[...] = jnp.zeros(acc_s.shape, F32)

    h2 = h2_s[...]
    a = jnp.dot(h2, wg_ref[...].astype(BF16), preferred_element_type=F32)
    u = jnp.dot(h2, wu_ref[...].astype(BF16), preferred_element_type=F32)
    y = jnp.dot((_silu(a) * u).astype(BF16), wd_ref[...].astype(BF16), preferred_element_type=F32)
    route = route_ref[...]
    lane = lax.broadcasted_iota(I32, route.shape, 1)
    gate = jnp.sum(jnp.where(lane == e + ROUTE_E0, route, 0.0), axis=-1, keepdims=True)
    acc_s[...] += gate * y

    @pl.when(e == N_EXPERTS - 1)
    def _last():
        y_ref[...] = _rmsnorm(x1_ref[...] + acc_s[...], gf_ref[...])


def _moe_dense_call(x1, route, g2, wg, wu, wd, gf, tm):
    n = x1.shape[0]
    return pl.pallas_call(
        _moe_dense_body,
        grid=(n // tm, N_EXPERTS),
        in_specs=[pl.BlockSpec((tm, D_MODEL), lambda i, e: (i, 0)),
                  pl.BlockSpec((tm, LANES), lambda i, e: (i, 0)),
                  pl.BlockSpec((1, D_MODEL), lambda i, e: (0, 0)),
                  pl.BlockSpec((None, D_MODEL, D_EXPERT), lambda i, e: (e, 0, 0)),
                  pl.BlockSpec((None, D_MODEL, D_EXPERT), lambda i, e: (e, 0, 0)),
                  pl.BlockSpec((None, D_EXPERT, D_MODEL), lambda i, e: (e, 0, 0)),
                  pl.BlockSpec((1, D_MODEL), lambda i, e: (0, 0))],
        out_specs=pl.BlockSpec((tm, D_MODEL), lambda i, e: (i, 0)),
        out_shape=jax.ShapeDtypeStruct((n, D_MODEL), F32),
        scratch_shapes=[pltpu.VMEM((tm, D_MODEL), BF16), pltpu.VMEM((tm, D_MODEL), F32)],
        compiler_params=pltpu.CompilerParams(
            dimension_semantics=("arbitrary", "arbitrary"),
            vmem_limit_bytes=VMEM_LIMIT_BYTES),
        name="moe_dense",
    )(x1, route, g2, wg, wu, wd, gf)


def _pick_block(seq, target):
    tb = min(seq, target)
    assert seq % tb == 0 and tb % CHUNK == 0
    return tb


def _lookup(table, idx):
    table = jnp.asarray(table, I32)
    k = jnp.arange(table.shape[0], dtype=I32)
    return jnp.sum(jnp.where(idx[..., None] == k, table, 0), axis=-1).astype(I32)


def _plan_steps(tab, counts, n_tiles):
    tile = jnp.arange(n_tiles, dtype=I32)
    n_used = tab[TAB_USED, 0]
    t_cls = tab[TAB_CLS, :n_tiles]
    t_ord = tab[TAB_ORD, :n_tiles]
    live = tile < n_used
    key = jnp.where(live, t_cls * n_tiles + tile, N_CLASSES * n_tiles + tile)
    pos = jnp.sum(key[None, :] < key[:, None], axis=1).astype(I32)
    step = jnp.arange(n_tiles + 2, dtype=I32)
    step_c = jnp.minimum(step, n_used - 1)
    step_tile = jnp.sum(jnp.where(pos[None, :] == step_c[:, None], tile[None, :], 0), axis=1).astype(I32)
    s_cls = _lookup(t_cls, step_tile)
    s_ord = _lookup(t_ord, step_tile)
    s_nv = jnp.clip(_lookup(counts, s_cls) - s_ord * MOE_TILE, 0, MOE_TILE).astype(I32)
    cls_ea = np.array([g * EXPERTS_PER_GROUP + a for g in range(N_GROUPS) for a, _ in PAIRS], np.int32)
    cls_eb = np.array([g * EXPERTS_PER_GROUP + b for g in range(N_GROUPS) for _, b in PAIRS], np.int32)
    return step_tile, _lookup(cls_ea, s_cls), _lookup(cls_eb, s_cls), s_nv, n_used.reshape(1).astype(I32)


def kernel(x_prompt, x_sample, cache_k, cache_v, state_hgrn, norm1_g, w_in, attn_sink, lower_bounds,
           hgrn_norm_g, w_o, norm2_g, w_router_group, b_router_group, w_router_expert, b_router_expert,
           w_gate, w_up, w_down, final_norm_g):
    depth = w_in.shape[0]
    assert depth == 1
    l = 0
    w_hist = cache_k.shape[2]
    assert w_hist == WINDOW
    w_r = jnp.concatenate(
        [w_router_group[l], jnp.transpose(w_router_expert[l], (1, 0, 2)).reshape(D_MODEL, N_EXPERTS)], axis=1)
    w_r = jnp.pad(w_r, ((0, 0), (0, CLS_ROWS - w_r.shape[1]))).T.astype(BF16)
    b_r = jnp.concatenate([b_router_group[l], b_router_expert[l].reshape(N_EXPERTS)])
    b_r = jnp.pad(b_r, (0, CLS_ROWS - b_r.shape[0])).reshape(CLS_ROWS, 1).astype(F32)
    weights = (norm1_g[l].reshape(1, D_MODEL), w_in[l].astype(BF16), attn_sink[l].astype(F32),
               lower_bounds.astype(F32), hgrn_norm_g[l].reshape(1, B_WIDTH), w_o[l].astype(BF16),
               norm2_g[l].reshape(1, D_MODEL), w_r, b_r)
    g2 = norm2_g[l].reshape(1, D_MODEL)
    gf = final_norm_g.reshape(1, D_MODEL)
    wg, wu, wd = w_gate[l], w_up[l], w_down[l]

    bp, lp, _ = x_prompt.shape
    bs, ls, _ = x_sample.shape
    np_tok, ns_tok = bp * lp, bs * ls
    tbp, tbs = _pick_block(lp, LAYER_BLOCK), _pick_block(ls, LAYER_BLOCK)
    assert tbp % MOE_TILE == 0 and np_tok % MOE_TILE == 0 and np_tok < TOK_SPLIT * TOK_SPLIT

    n_tiles = np_tok // MOE_TILE + N_CLASSES
    assert n_tiles <= TILE_LANES
    xs, tab, cnt, kp, vp, sp = _layer_call(x_prompt, None, weights, 0, tbp, n_tiles + tbp // MOE_TILE)
    counts = cnt[:N_CLASSES, 0].astype(I32)
    step_blk, step_ea, step_eb, step_nv, n_used = _plan_steps(tab, counts, n_tiles)
    yp = _moe_sorted_call(step_blk, step_ea, step_eb, step_nv, n_used, xs, g2, wg, wu, wd, gf, np_tok)

    cache = (cache_k[l].reshape(bs, w_hist, A_KV_WIDTH), cache_v[l].reshape(bs, w_hist, A_KV_WIDTH), state_hgrn[l])
    x1s, routes, kn, vn, sn = _layer_call(x_sample, cache, weights, PAST_LEN, tbs)
    ys = _moe_dense_call(x1s.reshape(ns_tok, D_MODEL), routes.reshape(ns_tok, LANES), g2, wg, wu, wd, gf,
                         _pick_block(ns_tok, LAYER_BLOCK))

    kv_shape = (1, -1, w_hist, A_KV_HEADS, A_HEAD_DIM)
    return (yp.reshape(bp, lp, D_MODEL), ys.reshape(bs, ls, D_MODEL),
            kp.reshape(kv_shape), vp.reshape(kv_shape), sp[None],
            kn.reshape(kv_shape), vn.reshape(kv_shape), sn[None])
```
